```python
import jax
import jax.numpy as jnp
from jax import lax
import numpy as np

D_MODEL = 2048
BATCH = 8
SEQ = 2048
DEPTH = 2
DEC_BATCH = 128
DEC_SEQ = 1
PAST_LEN = 2048
PAGE_SIZE = 128

RET_HEADS = 8
RET_QK_DIM = 128
RET_V_DIM = 256
RET_CHUNK = 128
RET_Q_W = RET_HEADS * RET_QK_DIM
RET_V_W = RET_HEADS * RET_V_DIM
ROPE_BASE = 10000.0

DIL_GROUPS = ((128, 1), (512, 4), (2048, 16))
DIL_HEADS = 4
DIL_HEAD_DIM = 128
DIL_W = DIL_HEADS * DIL_HEAD_DIM

IN_SIZES = (RET_Q_W, RET_Q_W, RET_V_W, RET_V_W) + (DIL_W,) * (3 * len(DIL_GROUPS)) + (D_MODEL, D_MODEL)
IN_WIDTH = sum(IN_SIZES)

N_EXPERTS = 32
TOP_K = 4
D_FF = D_MODEL
SWIGLU_LIMIT = 7.0
SWIGLU_ALPHA = 1.702
MOE_BLOCK = 128
EPS = 1e-6

kernel_name = 'retnet_dilated_moe_adaln_step'


def rms_norm(x, g):
    xf = x.astype(jnp.float32)
    y = xf * lax.rsqrt(jnp.mean(xf * xf, axis=-1, keepdims=True) + EPS)
    return (y * g.astype(jnp.float32)).astype(x.dtype)


def adaln(c, w, b):
    mod = jax.nn.silu(c) @ w + b
    shift, scale, gate = jnp.split(mod[:, None, :], 3, axis=-1)
    return shift, scale, gate


def rotary(x, pos):
    half = x.shape[-1] // 2
    inv = ROPE_BASE ** -jnp.linspace(0.0, 1.0, half, dtype=jnp.float32)
    ang = pos.astype(jnp.float32)[:, None] * inv[None, :]
    cos = jnp.cos(ang)[None, :, None, :]
    sin = jnp.sin(ang)[None, :, None, :]
    x1, x2 = x[..., :half], x[..., half:]
    return jnp.concatenate([x1 * cos - x2 * sin, x1 * sin + x2 * cos], axis=-1)


def retention(q, k, v, state0, chunk):
    B, T, H, Dk = q.shape
    Dv = v.shape[-1]
    n = T // chunk
    log_g = jnp.log1p(-jnp.exp2(-5.0 - jnp.arange(H, dtype=jnp.float32)))
    idx = jnp.arange(chunk, dtype=jnp.float32)
    rel = idx[:, None] - idx[None, :]
    inner_decay = jnp.where(rel >= 0, jnp.exp(log_g[:, None, None] * jnp.maximum(rel, 0.0)), 0.0)
    q_decay = jnp.exp(log_g[:, None] * (idx + 1.0)[None, :])
    k_decay = jnp.exp(log_g[:, None] * (chunk - 1.0 - idx)[None, :])
    chunk_decay = jnp.exp(log_g * chunk)

    def to_chunks(a):
        return a.reshape(B, n, chunk, H, a.shape[-1]).transpose(1, 0, 2, 3, 4)

    def step(S, inp):
        qc, kc, vc = inp
        s = jnp.einsum('bqhd,bkhd->bhqk', qc, kc) * inner_decay[None]
        o = (jnp.einsum('bhqk,bkhe->bqhe', s, vc)
             + jnp.einsum('bqhd,hq,bhde->bqhe', qc, q_decay, S))
        S = S * chunk_decay[None, :, None, None] + jnp.einsum('bkhd,hk,bkhe->bhde', kc, k_decay, vc)
        return S, o

    S, o = lax.scan(step, state0, (to_chunks(q), to_chunks(k), to_chunks(v)))
    o = o.transpose(1, 0, 2, 3, 4).reshape(B, T, H, Dv)
    return o, S


def dilated_attn_prompt(q, k, v, window, dil):
    B, S, H, E = q.shape
    band = window // dil
    L = S // dil
    nb = -(-L // band)
    Lp = nb * band

    def split(a):
        a = a.reshape(B, L, dil, H, E).transpose(0, 2, 1, 3, 4)
        a = jnp.pad(a, ((0, 0), (0, 0), (0, Lp - L), (0, 0), (0, 0)))
        return a.reshape(B, dil, nb, band, H, E)

    def with_prev(a):
        prev = jnp.pad(a, ((0, 0), (0, 0), (1, 0), (0, 0), (0, 0), (0, 0)))[:, :, :-1]
        return jnp.concatenate([prev, a], axis=3)

    qb = split(q)
    kk = with_prev(split(k))
    vv = with_prev(split(v))
    s = jnp.einsum('bpnqhe,bpnkhe->bpnhqk', qb, kk,
                   preferred_element_type=jnp.float32) * (DIL_HEAD_DIM ** -0.5)
    qi = jnp.arange(band)[:, None]
    kj = jnp.arange(2 * band)[None, :]
    dist = band + qi - kj
    blk = jnp.arange(nb)[:, None, None]
    valid = (dist >= 0) & (dist <= band) & ((blk > 0) | (kj >= band))
    s = jnp.where(valid[None, None, :, None], s, -jnp.inf)
    m = jnp.max(s, axis=-1)
    p = jnp.exp(s - m[..., None])
    l = jnp.sum(p, axis=-1)
    num = jnp.einsum('bpnhqk,bpnkhe->bpnqhe', p, vv.astype(jnp.float32))

    def unsplit(a):
        a = a.reshape((B, dil, Lp) + a.shape[4:])[:, :, :L]
        a = jnp.swapaxes(a, 1, 2)
        return a.reshape((B, S) + a.shape[3:])

    return unsplit(num), unsplit(jnp.moveaxis(m, 3, 4)), unsplit(jnp.moveaxis(l, 3, 4))


def dilated_attn_sample(q, k, v, kv_buf, window, dil):
    T = q.shape[1]
    Wb = kv_buf.shape[1]
    band = window // dil
    keys = jnp.concatenate([kv_buf[:, :, 0], k], axis=1)
    vals = jnp.concatenate([kv_buf[:, :, 1], v], axis=1)
    idx = Wb + jnp.arange(T)[:, None] - dil * jnp.arange(band + 1)[None, :]
    valid = idx >= 0
    idx = jnp.maximum(idx, 0)
    kg = keys[:, idx]
    vg = vals[:, idx]
    s = jnp.einsum('bthe,btjhe->bthj', q, kg,
                   preferred_element_type=jnp.float32) * (DIL_HEAD_DIM ** -0.5)
    s = jnp.where(valid[None, :, None, :], s, -jnp.inf)
    m = jnp.max(s, axis=-1)
    p = jnp.exp(s - m[..., None])
    l = jnp.sum(p, axis=-1)
    num = jnp.einsum('bthj,btjhe->bthe', p, vg.astype(jnp.float32))
    return num, m, l


def combine_by_denominator(partials):
    m_all = partials[0][1]
    for _, m_g, _ in partials[1:]:
        m_all = jnp.maximum(m_all, m_g)
    num = 0.0
    den = 0.0
    for n_g, m_g, l_g in partials:
        w = jnp.exp(m_g - m_all)
        num = num + n_g * w[..., None]
        den = den + l_g * w
    return num / den[..., None]


def token_mixer(h, pos, ret_state, kv_bufs, w_in, w_out_ret, w_out_dil, w_out):
    B, T, _ = h.shape
    f32 = jnp.float32
    cuts = [int(c) for c in np.cumsum(IN_SIZES)[:-1]]
    parts = jnp.split(h @ w_in, cuts, axis=-1)
    q_r, k_r, v_r, g_r = parts[:4]
    dil_parts = parts[4:4 + 3 * len(DIL_GROUPS)]
    gate_a, gate_b = parts[-2], parts[-1]

    q = rotary(q_r.reshape(B, T, RET_HEADS, RET_QK_DIM).astype(f32), pos)
    k = rotary(k_r.reshape(B, T, RET_HEADS, RET_QK_DIM).astype(f32), pos) * (RET_QK_DIM ** -0.5)
    v = v_r.reshape(B, T, RET_HEADS, RET_V_DIM).astype(f32)
    if ret_state is None:
        s0 = jnp.zeros((B, RET_HEADS, RET_QK_DIM, RET_V_DIM), f32)
    else:
        s0 = ret_state.astype(f32)
    chunk = RET_CHUNK if T % RET_CHUNK == 0 else T
    o, s_new = retention(q, k, v, s0, chunk)
    o = o * lax.rsqrt(jnp.mean(o * o, axis=-1, keepdims=True) + EPS)
    y_a = (jax.nn.silu(g_r) * o.reshape(B, T, RET_V_W).astype(h.dtype)) @ w_out_ret

    partials = []
    new_kv = []
    for g, (window, dil) in enumerate(DIL_GROUPS):
        qg, kg, vg = (a.reshape(B, T, DIL_HEADS, DIL_HEAD_DIM) for a in dil_parts[3 * g:3 * g + 3])
        if kv_bufs is None:
            partials.append(dilated_attn_prompt(qg, kg, vg, window, dil))
            new_kv.append(jnp.stack([kg, vg], axis=2)[:, T - min(window, T):])
        else:
            partials.append(dilated_attn_sample(qg, kg, vg, kv_bufs[g], window, dil))
            new_kv.append(jnp.stack([kg, vg], axis=2))
    y_b = combine_by_denominator(partials).reshape(B, T, DIL_W).astype(h.dtype) @ w_out_dil

    merged = jax.nn.sigmoid(gate_a) * y_a + jax.nn.sigmoid(gate_b) * y_b
    return merged @ w_out, s_new.astype(h.dtype), new_kv


def moe_ffn(h, router_w, router_b, w_gate_up, b_gate_up, w_down, b_down):
    lead = h.shape[:-1]
    x = h.reshape(-1, D_MODEL)
    N = x.shape[0]
    logits = (x @ router_w + router_b).astype(jnp.float32)
    top_val, top_idx = lax.top_k(logits, TOP_K)
    weights = jax.nn.softmax(top_val, axis=-1).astype(x.dtype)
    A = N * TOP_K
    flat_e = top_idx.reshape(-1)
    flat_tok = jnp.broadcast_to(jnp.arange(N, dtype=jnp.int32)[:, None], (N, TOP_K)).reshape(-1)
    flat_w = weights.reshape(-1)
    order = jnp.argsort(flat_e)
    e_sorted = flat_e[order]
    counts = jnp.zeros((N_EXPERTS,), jnp.int32).at[flat_e].add(1)
    padded = ((counts + MOE_BLOCK - 1) // MOE_BLOCK) * MOE_BLOCK
    start = jnp.cumsum(counts) - counts
    pstart = jnp.cumsum(padded) - padded
    dest = pstart[e_sorted] + jnp.arange(A, dtype=jnp.int32) - start[e_sorted]
    n_blocks = -(-A // MOE_BLOCK) + N_EXPERTS
    P = n_blocks * MOE_BLOCK
    row_tok = jnp.zeros((P,), jnp.int32).at[dest].set(flat_tok[order])
    row_w = jnp.zeros((P,), x.dtype).at[dest].set(flat_w[order])
    block_end = jnp.cumsum(padded) // MOE_BLOCK
    block_exp = jnp.minimum(jnp.searchsorted(block_end, jnp.arange(n_blocks, dtype=jnp.int32), side='right'),
                            N_EXPERTS - 1)
    xs = x[row_tok].reshape(n_blocks, MOE_BLOCK, D_MODEL)

    def expert_block(args):
        xb, e = args
        gu = xb @ w_gate_up[e] + b_gate_up[e]
        gate, up = gu[:, :D_FF], gu[:, D_FF:]
        gate = jnp.minimum(gate, SWIGLU_LIMIT)
        up = jnp.clip(up, -SWIGLU_LIMIT, SWIGLU_LIMIT)
        act = (up + 1.0) * gate * jax.nn.sigmoid(SWIGLU_ALPHA * gate)
        return act @ w_down[e] + b_down[e]

    ys = lax.map(expert_block, (xs, block_exp))
    out = jnp.zeros_like(x).at[row_tok].add(ys.reshape(P, D_MODEL) * row_w[:, None])
    return out.reshape(lead + (D_MODEL,))


def decoder_layer(x, c, pos, ret_state, kv_bufs, norm1_g, ada1_w, ada1_b, w_in, w_out_ret, w_out_dil,
                  w_out, norm2_g, ada2_w, ada2_b, router_w, router_b, w_gate_up, b_gate_up, w_down, b_down):
    shift, scale, gate = adaln(c, ada1_w, ada1_b)
    h = rms_norm(x, norm1_g) * (1.0 + scale) + shift
    mix, s_new, new_kv = token_mixer(h, pos, ret_state, kv_bufs, w_in, w_out_ret, w_out_dil, w_out)
    x = x + gate * mix
    shift, scale, gate = adaln(c, ada2_w, ada2_b)
    h = rms_norm(x, norm2_g) * (1.0 + scale) + shift
    x = x + gate * moe_ffn(h, router_w, router_b, w_gate_up, b_gate_up, w_down, b_down)
    return x, s_new, new_kv


def setup_inputs(seed: int = 0) -> dict:
    key = jax.random.key(seed)
    ks = jax.random.split(key, 32)
    f32 = jnp.float32

    def nrm(i, shape, scale):
        return jax.random.normal(ks[i], shape, f32) * scale

    def kv_shape(window):
        return (DEPTH, DEC_BATCH, min(window, PAST_LEN), 2, DIL_HEADS, DIL_HEAD_DIM)

    D = D_MODEL
    return {
        'x_prompt': nrm(0, (BATCH, SEQ, D), 1.0),
        'x_sample': nrm(1, (DEC_BATCH, DEC_SEQ, D), 1.0),
        'state_ret': nrm(2, (DEPTH, DEC_BATCH, RET_HEADS, RET_QK_DIM, RET_V_DIM), 0.3),
        'cache_kv_w128_d1': nrm(3, kv_shape(DIL_GROUPS[0][0]), 1.0),
        'cache_kv_w512_d4': nrm(4, kv_shape(DIL_GROUPS[1][0]), 1.0),
        'cache_kv_w2048_d16': nrm(5, kv_shape(DIL_GROUPS[2][0]), 1.0),
        'c_prompt': nrm(6, (BATCH, D), 1.0),
        'c_sample': nrm(7, (DEC_BATCH, D), 1.0),
        'norm1_g': 1.0 + nrm(8, (DEPTH, D), 0.02),
        'ada1_w': nrm(9, (DEPTH, D, 3 * D), 0.5 * D ** -0.5),
        'ada1_b': nrm(10, (DEPTH, 3 * D), 0.02),
        'w_in': nrm(11, (DEPTH, D, IN_WIDTH), D ** -0.5),
        'w_out_ret': nrm(12, (DEPTH, RET_V_W, D), RET_V_W ** -0.5),
        'w_out_dil': nrm(13, (DEPTH, DIL_W, D), DIL_W ** -0.5),
        'w_out': nrm(14, (DEPTH, D, D), D ** -0.5),
        'norm2_g': 1.0 + nrm(15, (DEPTH, D), 0.02),
        'ada2_w': nrm(16, (DEPTH, D, 3 * D), 0.5 * D ** -0.5),
        'ada2_b': nrm(17, (DEPTH, 3 * D), 0.02),
        'router_w': nrm(18, (DEPTH, D, N_EXPERTS), D ** -0.5),
        'router_b': nrm(19, (DEPTH, N_EXPERTS), 0.01),
        'w_gate_up': nrm(20, (DEPTH, N_EXPERTS, D, 2 * D_FF), D ** -0.5),
        'b_gate_up': nrm(21, (DEPTH, N_EXPERTS, 2 * D_FF), 0.01),
        'w_down': nrm(22, (DEPTH, N_EXPERTS, D_FF, D), D_FF ** -0.5),
        'b_down': nrm(23, (DEPTH, N_EXPERTS, D), 0.01),
        'final_norm_g': 1.0 + nrm(24, (D,), 0.02),
    }


def reference(x_prompt, x_sample, state_ret, cache_kv_w128_d1, cache_kv_w512_d4, cache_kv_w2048_d16,
              c_prompt, c_sample, norm1_g, ada1_w, ada1_b, w_in, w_out_ret, w_out_dil, w_out,
              norm2_g, ada2_w, ada2_b, router_w, router_b, w_gate_up, b_gate_up, w_down, b_down,
              final_norm_g):
    pos_p = jnp.arange(x_prompt.shape[1], dtype=jnp.int32)
    pos_s = PAST_LEN + jnp.arange(x_sample.shape[1], dtype=jnp.int32)
    caches = (cache_kv_w128_d1, cache_kv_w512_d4, cache_kv_w2048_d16)
    hp, hs = x_prompt, x_sample
    ret_p, ret_s = [], []
    kv_p = [[] for _ in DIL_GROUPS]
    kv_s = [[] for _ in DIL_GROUPS]
    for layer in range(DEPTH):
        lw = (norm1_g[layer], ada1_w[layer], ada1_b[layer], w_in[layer], w_out_ret[layer], w_out_dil[layer],
              w_out[layer], norm2_g[layer], ada2_w[layer], ada2_b[layer], router_w[layer], router_b[layer],
              w_gate_up[layer], b_gate_up[layer], w_down[layer], b_down[layer])
        hp, sp, nkp = decoder_layer(hp, c_prompt, pos_p, None, None, *lw)
        hs, ss, nks = decoder_layer(hs, c_sample, pos_s, state_ret[layer],
                                    tuple(cc[layer] for cc in caches), *lw)
        ret_p.append(sp)
        ret_s.append(ss)
        for g in range(len(DIL_GROUPS)):
            kv_p[g].append(nkp[g])
            kv_s[g].append(nks[g])
    y_prompt = rms_norm(hp, final_norm_g)
    y_sample = rms_norm(hs, final_norm_g)
    return (y_prompt, y_sample,
            jnp.stack(ret_p), jnp.stack(kv_p[0]), jnp.stack(kv_p[1]), jnp.stack(kv_p[2]),
            jnp.stack(ret_s), jnp.stack(kv_s[0]), jnp.stack(kv_s[1]), jnp.stack(kv_s[2]))
```

```python
import functools

import jax
import jax.numpy as jnp
from jax import lax
from jax.experimental import pallas as pl
from jax.experimental.pallas import tpu as pltpu

F32 = jnp.float32
BF16 = jnp.bfloat16
U32 = jnp.uint32
I32 = jnp.int32

D = 2048
N_LAYERS = 2
RET_H = 8
RET_DK = 128
RET_DV = 256
RET_C = 128
ROPE_BASE = 10000.0
DIL = ((128, 1), (512, 4), (2048, 16))
DIL_H = 4
DIL_E = 128
DIL_W = DIL_H * DIL_E
N_EXP = 32
TOPK = 4
DFF = 2048
SW_LIMIT = 7.0
SW_ALPHA = 1.702
EPS = 1e-6

OFF_Q = 0
OFF_K = 1024
OFF_V = 2048
OFF_G = 4096
OFF_DIL = 6144
OFF_GA = 10752
OFF_GB = 12800
IN_W = 14848

VMEM_LIMIT = 56 * 1024 * 1024
MOE_TM = 256
ROUTE_CHUNK = 1024
TOK_CHUNK = ROUTE_CHUNK // TOPK


def _cp(n_grid_dims):
    return pltpu.CompilerParams(
        dimension_semantics=("arbitrary",) * n_grid_dims,
        vmem_limit_bytes=VMEM_LIMIT)


def _row_tile(n, cap):
    for k in range(1, n + 1):
        if n % k == 0 and n // k <= cap and (n // k) % 16 == 0:
            return n // k
    raise ValueError((n, cap))


def _silu(x):
    return x * jax.nn.sigmoid(x)


def _mods_body(c_ref, w_ref, b_ref, o_ref):
    a = _silu(c_ref[...]).astype(BF16)
    o_ref[...] = jnp.dot(a, w_ref[...].astype(BF16), preferred_element_type=F32) + b_ref[...]


def _mods(c_all, ada_w, ada_b):
    nc = c_all.shape[0]
    tn = 1024
    return pl.pallas_call(
        _mods_body,
        grid=(N_LAYERS, 3 * D // tn),
        in_specs=[pl.BlockSpec((nc, D), lambda l, n: (0, 0)),
                  pl.BlockSpec((None, D, tn), lambda l, n: (l, 0, n)),
                  pl.BlockSpec((None, 1, tn), lambda l, n: (l, 0, n))],
        out_specs=pl.BlockSpec((None, nc, tn), lambda l, n: (l, 0, n)),
        out_shape=jax.ShapeDtypeStruct((N_LAYERS, nc, 3 * D), F32),
        compiler_params=_cp(2), name="ada_mods",
    )(c_all, ada_w, ada_b.reshape(N_LAYERS, 1, 3 * D))


class _Mods:
    def __init__(self, m, ns, nb, seq):
        self.m3 = m
        self.m4 = m.reshape(m.shape[0], m.shape[1], 1, 3 * D)
        self.ns, self.nb, self.seq = ns, nb, seq

    @property
    def args(self):
        return (self.m4, self.m3)

    def specs(self, layer, which, tr, tn=D, col_axis=False):
        per = self.seq // tr
        ns, nb = self.ns, self.nb
        nblk = D // tn
        if col_axis:
            p_map = lambda n, i: (layer, ns + jnp.minimum(i // per, nb - 1), 0, which * nblk + n)
            s_map = lambda n, i: (layer, 0, which * nblk + n)
        else:
            p_map = lambda i: (layer, ns + jnp.minimum(i // per, nb - 1), 0, which * nblk)
            s_map = lambda i: (layer, 0, which * nblk)
        return [pl.BlockSpec((None, None, 1, tn), p_map), pl.BlockSpec((None, ns, tn), s_map)]


def _prompt_or_sample(step, n_steps, prompt_fn, sample_fn):
    pl.when(step < n_steps - 1)(prompt_fn)
    pl.when(step == n_steps - 1)(sample_fn)


def _normed(x, g):
    return x * lax.rsqrt(jnp.mean(x * x, axis=-1, keepdims=True) + EPS) * g


def _norm_mod_body(x_ref, g_ref, shp_ref, scp_ref, shs_ref, scs_ref, o_ref, *, ns):
    def prompt():
        y = _normed(x_ref[...], g_ref[...])
        o_ref[...] = (y * (1.0 + scp_ref[...]) + shp_ref[...]).astype(o_ref.dtype)

    def sample():
        y = _normed(x_ref[:ns], g_ref[...])
        o_ref[:ns] = (y * (1.0 + scs_ref[...]) + shs_ref[...]).astype(o_ref.dtype)
        o_ref[ns:] = jnp.zeros_like(o_ref[ns:])

    _prompt_or_sample(pl.program_id(0), pl.num_programs(0), prompt, sample)


def _norm_mod(x_all, g, mods, layer, np_rows, ns):
    nt = x_all.shape[0]
    tr = 1024
    shp, shs = mods.specs(layer, 0, tr)
    scp, scs = mods.specs(layer, 1, tr)
    return pl.pallas_call(
        functools.partial(_norm_mod_body, ns=ns),
        grid=(np_rows // tr + 1,),
        in_specs=[pl.BlockSpec((tr, D), lambda i: (i, 0)),
                  pl.BlockSpec((None, 1, D), lambda i: (layer, 0, 0)),
                  shp, scp, shs, scs],
        out_specs=pl.BlockSpec((tr, D), lambda i: (i, 0)),
        out_shape=jax.ShapeDtypeStruct((nt, D), BF16),
        compiler_params=_cp(1), name="norm1",
    )(x_all, g, mods.m4, mods.m4, mods.m3, mods.m3)


def _proj_body(x_ref, w_ref, o_ref, wbf_ref):
    @pl.when(pl.program_id(1) == 0)
    def _():
        wbf_ref[...] = w_ref[...].astype(BF16)

    o_ref[...] = jnp.dot(x_ref[...], wbf_ref[...], preferred_element_type=F32).astype(o_ref.dtype)


def _proj(h, w_in, layer):
    nt = h.shape[0]
    tm = _row_tile(nt, 1376)
    tn = 512
    return pl.pallas_call(
        _proj_body,
        grid=(IN_W // tn, nt // tm),
        in_specs=[pl.BlockSpec((tm, D), lambda n, m: (m, 0)),
                  pl.BlockSpec((None, D, tn), lambda n, m: (layer, 0, n))],
        out_specs=pl.BlockSpec((tm, tn), lambda n, m: (m, n)),
        out_shape=jax.ShapeDtypeStruct((nt, IN_W), F32),
        scratch_shapes=[pltpu.VMEM((D, tn), BF16)],
        compiler_params=_cp(2), name="in_proj",
    )(h, w_in)


def _rope_tables(pos):
    half = RET_DK // 2
    inv = ROPE_BASE ** -jnp.linspace(0.0, 1.0, half, dtype=F32)
    ang = pos.astype(F32)[:, None] * inv[None, :]
    cos = jnp.cos(ang)
    sin = jnp.sin(ang)
    return jnp.concatenate([cos, cos], axis=-1), jnp.concatenate([-sin, sin], axis=-1)


def _decay_tables(chunk):
    log_g = jnp.log1p(-jnp.exp2(-5.0 - jnp.arange(RET_H, dtype=F32)))
    idx = jnp.arange(chunk, dtype=F32)
    rel = idx[:, None] - idx[None, :]
    inner = jnp.where(rel >= 0, jnp.exp(log_g[:, None, None] * jnp.maximum(rel, 0.0)), 0.0)
    q_decay = jnp.exp(log_g[:, None] * (idx + 1.0)[None, :])
    k_decay = jnp.exp(log_g[:, None] * (chunk - 1.0 - idx)[None, :])
    chunk_decay = jnp.exp(log_g * chunk)
    return inner, q_decay, k_decay, chunk_decay


def _rot(x, cos, sin):
    return x * cos + pltpu.roll(x, RET_DK // 2, 1) * sin


def _head_norm_gate(o, g):
    on = o * lax.rsqrt(jnp.mean(o * o, axis=-1, keepdims=True) + EPS)
    return _silu(g) * on


def _ret_prompt_body(q_ref, k_ref, v_ref, g_ref, cos_ref, sin_ref, inner_ref, qd_ref, kd_ref, cd_ref,
                     ya_ref, s_ref):
    @pl.when(pl.program_id(1) == 0)
    def _():
        s_ref[...] = jnp.zeros_like(s_ref)

    cos = cos_ref[...]
    sin = sin_ref[...]
    for h in range(RET_H):
        q = _rot(q_ref[:, h * RET_DK:(h + 1) * RET_DK], cos, sin)
        k = _rot(k_ref[:, h * RET_DK:(h + 1) * RET_DK], cos, sin) * (RET_DK ** -0.5)
        vb = v_ref[:, h * RET_DV:(h + 1) * RET_DV].astype(BF16)
        s = lax.dot_general(q.astype(BF16), k.astype(BF16), (((1,), (1,)), ((), ())),
                            preferred_element_type=F32) * inner_ref[h]
        state = s_ref[h]
        o = (jnp.dot(s.astype(BF16), vb, preferred_element_type=F32)
             + jnp.dot((q * qd_ref[h]).astype(BF16), state.astype(BF16), preferred_element_type=F32))
        kt = (k * kd_ref[h]).T.astype(BF16)
        s_ref[h] = state * cd_ref[h] + jnp.dot(kt, vb, preferred_element_type=F32)
        y = _head_norm_gate(o, g_ref[:, h * RET_DV:(h + 1) * RET_DV])
        ya_ref[:, h * RET_DV:(h + 1) * RET_DV] = y.astype(ya_ref.dtype)


def _ret_prompt(proj, nb, seq):
    nch = seq // RET_C
    cos, sin = _rope_tables(jnp.arange(seq, dtype=I32))
    inner, qd, kd, cd = _decay_tables(RET_C)
    qd_b = jnp.broadcast_to(qd[:, :, None], (RET_H, RET_C, RET_DK))
    kd_b = jnp.broadcast_to(kd[:, :, None], (RET_H, RET_C, RET_DK))
    cd_b = jnp.broadcast_to(cd[:, None, None], (RET_H, 1, RET_DV))
    row = lambda b, c: b * nch + c
    full3 = lambda shp: pl.BlockSpec(shp, lambda b, c: (0, 0, 0))
    return pl.pallas_call(
        _ret_prompt_body,
        grid=(nb, nch),
        in_specs=[pl.BlockSpec((RET_C, 1024), lambda b, c: (row(b, c), OFF_Q // 1024)),
                  pl.BlockSpec((RET_C, 1024), lambda b, c: (row(b, c), OFF_K // 1024)),
                  pl.BlockSpec((RET_C, 2048), lambda b, c: (row(b, c), OFF_V // 2048)),
                  pl.BlockSpec((RET_C, 2048), lambda b, c: (row(b, c), OFF_G // 2048)),
                  pl.BlockSpec((RET_C, RET_DK), lambda b, c: (c, 0)),
                  pl.BlockSpec((RET_C, RET_DK), lambda b, c: (c, 0)),
                  full3((RET_H, RET_C, RET_C)), full3((RET_H, RET_C, RET_DK)),
                  full3((RET_H, RET_C, RET_DK)), full3((RET_H, 1, RET_DV))],
        out_specs=[pl.BlockSpec((RET_C, D), lambda b, c: (row(b, c), 0)),
                   pl.BlockSpec((None, RET_H, RET_DK, RET_DV), lambda b, c: (b, 0, 0, 0))],
        out_shape=[jax.ShapeDtypeStruct((nb * seq, D), BF16),
                   jax.ShapeDtypeStruct((nb, RET_H, RET_DK, RET_DV), F32)],
        compiler_params=_cp(2), name="retention_prompt",
    )(proj, proj, proj, proj, cos, sin, inner, qd_b, kd_b, cd_b)


RS_NB = 8


def _ret_sample_body(q_ref, k_ref, v_ref, g_ref, cos_ref, sin_ref, inner_ref, qd_ref, kd_ref, cd_ref,
                     st_ref, ya_ref, so_ref):
    i = pl.program_id(0)
    cos = cos_ref[...]
    sin = sin_ref[...]
    row_i = lax.broadcasted_iota(I32, (RET_DK, RET_DK), 0)
    col_i = lax.broadcasted_iota(I32, (RET_DK, RET_DK), 1)
    eye = row_i == col_i
    rows = pl.ds(pl.multiple_of(i * RS_NB, RS_NB), RS_NB)
    q_all = q_ref[rows, :]
    k_all = k_ref[rows, :]
    v_all = v_ref[rows, :]
    g_all = g_ref[rows, :]
    out_rows = []
    for j in range(RS_NB):
        heads = []
        for h in range(RET_H):
            q = _rot(q_all[j:j + 1, h * RET_DK:(h + 1) * RET_DK], cos, sin)
            k = _rot(k_all[j:j + 1, h * RET_DK:(h + 1) * RET_DK], cos, sin) * (RET_DK ** -0.5)
            v = v_all[j:j + 1, h * RET_DV:(h + 1) * RET_DV]
            s = jnp.sum(q * k, axis=-1, keepdims=True) * inner_ref[h]
            state = st_ref[j, h]
            q8 = jnp.broadcast_to(q * qd_ref[h], (8, RET_DK)).astype(BF16)
            qs = jnp.dot(q8, state.astype(BF16), preferred_element_type=F32)[0:1]
            o = s * v + qs
            kdiag = jnp.where(eye, jnp.broadcast_to(k * kd_ref[h], (RET_DK, RET_DK)), 0.0).astype(BF16)
            vb = jnp.broadcast_to(v, (RET_DK, RET_DV)).astype(BF16)
            so_ref[j, h] = state * cd_ref[h] + jnp.dot(kdiag, vb, preferred_element_type=F32)
            heads.append(_head_norm_gate(o, g_all[j:j + 1, h * RET_DV:(h + 1) * RET_DV]))
        out_rows.append(jnp.concatenate(heads, axis=1))
    ya_ref[...] = jnp.concatenate(out_rows, axis=0)


def _ret_sample(proj, state_ret, layer, np_rows, ns, past):
    cos, sin = _rope_tables(past + jnp.arange(1, dtype=I32))
    inner, qd, kd, cd = _decay_tables(1)
    inner_b = inner.reshape(RET_H, 1, 1)
    qd_b = jnp.broadcast_to(qd[:, :, None], (RET_H, 1, RET_DK))
    kd_b = jnp.broadcast_to(kd[:, :, None], (RET_H, 1, RET_DK))
    cd_b = jnp.broadcast_to(cd[:, None, None], (RET_H, 1, RET_DV))
    sblk = np_rows // ns
    full = lambda shp: pl.BlockSpec(shp, lambda i: (0,) * len(shp))
    return pl.pallas_call(
        _ret_sample_body,
        grid=(ns // RS_NB,),
        in_specs=[pl.BlockSpec((ns, 1024), lambda i: (sblk, OFF_Q // 1024)),
                  pl.BlockSpec((ns, 1024), lambda i: (sblk, OFF_K // 1024)),
                  pl.BlockSpec((ns, 2048), lambda i: (sblk, OFF_V // 2048)),
                  pl.BlockSpec((ns, 2048), lambda i: (sblk, OFF_G // 2048)),
                  full((1, RET_DK)), full((1, RET_DK)),
                  full((RET_H, 1, 1)), full((RET_H, 1, RET_DK)), full((RET_H, 1, RET_DK)),
                  full((RET_H, 1, RET_DV)),
                  pl.BlockSpec((None, RS_NB, RET_H, RET_DK, RET_DV), lambda i: (layer, i, 0, 0, 0))],
        out_specs=[pl.BlockSpec((RS_NB, D), lambda i: (i, 0)),
                   pl.BlockSpec((RS_NB, RET_H, RET_DK, RET_DV), lambda i: (i, 0, 0, 0))],
        out_shape=[jax.ShapeDtypeStruct((ns, D), F32),
                   jax.ShapeDtypeStruct((ns, RET_H, RET_DK, RET_DV), F32)],
        compiler_params=_cp(1), name="retention_sample",
    )(proj, proj, proj, proj, cos, sin, inner_b, qd_b, kd_b, cd_b, state_ret)


def _merge_groups(parts):
    m_all = parts[0][1]
    for _, m_g, _ in parts[1:]:
        m_all = jnp.maximum(m_all, m_g)
    num = 0.0
    den = 0.0
    for n_g, m_g, l_g in parts:
        w = jnp.exp(m_g - m_all)
        num = num + n_g * w
        den = den + l_g * w
    return num / den


def _dil_prompt_body(*refs, seq):
    qkv = refs[:9]
    yb_ref = refs[9]
    num_ref, m_ref, l_ref = refs[10:13]
    band = 128
    scale = DIL_E ** -0.5
    qi = lax.broadcasted_iota(I32, (band, band), 0)
    kj1 = lax.broadcasted_iota(I32, (band, band), 1)
    qi2 = lax.broadcasted_iota(I32, (band, 2 * band), 0)
    kj2 = lax.broadcasted_iota(I32, (band, 2 * band), 1)
    first_valid = kj1 <= qi
    later_valid = (kj2 >= qi2) & (kj2 <= qi2 + band)
    for g, (window, dil) in enumerate(DIL):
        assert window // dil == band
        q_ref, k_ref, v_ref = qkv[3 * g:3 * g + 3]
        nblk = seq // dil // band
        for p in range(dil):
            for n in range(nblk):
                rows_q = pl.ds(p + n * band * dil, band, stride=dil)
                qb = q_ref[rows_q, :].astype(BF16)
                if n == 0:
                    rows_k = rows_q
                    valid = first_valid
                else:
                    rows_k = pl.ds(p + (n - 1) * band * dil, 2 * band, stride=dil)
                    valid = later_valid
                kk = k_ref[rows_k, :].astype(BF16)
                vv = v_ref[rows_k, :].astype(BF16)
                s = lax.dot_general(qb, kk, (((1,), (1,)), ((), ())), preferred_element_type=F32) * scale
                s = jnp.where(valid, s, -jnp.inf)
                m = jnp.max(s, axis=-1, keepdims=True)
                pe = jnp.exp(s - m)
                l = jnp.sum(pe, axis=-1, keepdims=True)
                num_ref[g, rows_q, :] = jnp.dot(pe.astype(BF16), vv, preferred_element_type=F32)
                m_ref[g, rows_q, :] = jnp.broadcast_to(m, (band, DIL_E))
                l_ref[g, rows_q, :] = jnp.broadcast_to(l, (band, DIL_E))
    parts = [(num_ref[g], m_ref[g], l_ref[g]) for g in range(len(DIL))]
    yb_ref[...] = _merge_groups(parts).astype(yb_ref.dtype)


def _dil_prompt(proj, nb, seq):
    in_specs = []
    for g in range(len(DIL)):
        for j in range(3):
            cb = (OFF_DIL + g * 3 * DIL_W + j * DIL_W) // DIL_E
            in_specs.append(pl.BlockSpec((seq, DIL_E), lambda b, h, cb=cb: (b, cb + h)))
    scr = pltpu.VMEM((len(DIL), seq, DIL_E), F32)
    return pl.pallas_call(
        functools.partial(_dil_prompt_body, seq=seq),
        grid=(nb, DIL_H),
        in_specs=in_specs,
        out_specs=pl.BlockSpec((seq, DIL_E), lambda b, h: (b, h)),
        out_shape=jax.ShapeDtypeStruct((nb * seq, DIL_W), BF16),
        scratch_shapes=[scr, scr, scr],
        compiler_params=_cp(2), name="dilated_prompt",
    )(*([proj] * 9))


DS_NB = 8


def _dil_sample_body(c0_ref, c1_ref, c2_ref, q0_ref, q1_ref, q2_ref, yb_ref):
    i = pl.program_id(0)
    caches = (c0_ref, c1_ref, c2_ref)
    scale = DIL_E ** -0.5
    rows = pl.ds(pl.multiple_of(i * DS_NB, DS_NB), DS_NB)
    news = [ref[rows, :] for ref in (q0_ref, q1_ref, q2_ref)]
    out_rows = []
    for j in range(DS_NB):
        parts = []
        for g in range(len(DIL)):
            new = news[g][j:j + 1]
            q4 = jnp.concatenate([new[:, h * DIL_E:(h + 1) * DIL_E] for h in range(DIL_H)], axis=0)
            k4 = jnp.concatenate([new[:, DIL_W + h * DIL_E:DIL_W + (h + 1) * DIL_E] for h in range(DIL_H)], axis=0)
            v4 = jnp.concatenate([new[:, 2 * DIL_W + h * DIL_E:2 * DIL_W + (h + 1) * DIL_E] for h in range(DIL_H)],
                                 axis=0)
            kc = caches[g][j, :, 0:DIL_H, :]
            vc = caches[g][j, :, DIL_H:2 * DIL_H, :]
            s_c = jnp.sum(kc * q4[None], axis=-1, keepdims=True) * scale
            s_n = jnp.sum(k4 * q4, axis=-1, keepdims=True) * scale
            m = jnp.maximum(jnp.max(s_c, axis=0), s_n)
            p_c = jnp.exp(s_c - m[None])
            p_n = jnp.exp(s_n - m)
            l = jnp.sum(p_c, axis=0) + p_n
            num = jnp.sum(p_c * vc, axis=0) + p_n * v4
            parts.append((num, m, l))
        y = _merge_groups(parts)
        out_rows.append(jnp.concatenate([y[h:h + 1] for h in range(DIL_H)], axis=1))
    yb_ref[...] = jnp.concatenate(out_rows, axis=0)


def _dil_sample(proj, caches, layer, np_rows, ns):
    sblk = np_rows // ns
    views = []
    in_specs = []
    for cache, (window, dil) in zip(caches, DIL):
        wb = cache.shape[2]
        assert wb == window and wb % dil == 0
        views.append(cache.reshape(cache.shape[0] * ns, wb // dil, dil, 2 * DIL_H, DIL_E))
        in_specs.append(pl.BlockSpec((DS_NB, wb // dil, None, 2 * DIL_H, DIL_E),
                                     lambda i: (layer * (ns // DS_NB) + i, 0, 0, 0, 0)))
    for g in range(len(DIL)):
        in_specs.append(pl.BlockSpec((ns, 3 * DIL_W), lambda i, g=g: (sblk, OFF_DIL // (3 * DIL_W) + g)))
    return pl.pallas_call(
        _dil_sample_body,
        grid=(ns // DS_NB,),
        in_specs=in_specs,
        out_specs=pl.BlockSpec((DS_NB, DIL_W), lambda i: (i, 0)),
        out_shape=jax.ShapeDtypeStruct((ns, DIL_W), F32),
        compiler_params=_cp(1), name="dilated_sample",
    )(*views, proj, proj, proj)


def _merge_body(yap_ref, ybp_ref, yas_ref, ybs_ref, wr_ref, wd_ref, ga_ref, gb_ref, o_ref, wr_bf, wd_bf, *, ns):
    @pl.when(pl.program_id(1) == 0)
    def _():
        wr_bf[...] = wr_ref[...].astype(BF16)
        wd_bf[...] = wd_ref[...].astype(BF16)

    def mixed(ya, yb, ga, gb):
        y_a = jnp.dot(ya, wr_bf[...], preferred_element_type=F32)
        y_b = jnp.dot(yb, wd_bf[...], preferred_element_type=F32)
        return (jax.nn.sigmoid(ga) * y_a + jax.nn.sigmoid(gb) * y_b).astype(o_ref.dtype)

    def prompt():
        o_ref[...] = mixed(yap_ref[...], ybp_ref[...], ga_ref[...], gb_ref[...])

    def sample():
        o_ref[:ns] = mixed(yas_ref[...].astype(BF16), ybs_ref[...].astype(BF16), ga_ref[:ns], gb_ref[:ns])
        o_ref[ns:] = jnp.zeros_like(o_ref[ns:])

    _prompt_or_sample(pl.program_id(1), pl.num_programs(1), prompt, sample)


def _merge(ya_p, yb_p, ya_s, yb_s, proj, w_out_ret, w_out_dil, layer):
    nt = proj.shape[0]
    np_rows = ya_p.shape[0]
    ns = ya_s.shape[0]
    tm = 1024
    tn = 512
    npt = np_rows // tm
    ptile = lambda m: jnp.minimum(m, npt - 1)
    return pl.pallas_call(
        functools.partial(_merge_body, ns=ns),
        grid=(D // tn, npt + 1),
        in_specs=[pl.BlockSpec((tm, D), lambda n, m: (ptile(m), 0)),
                  pl.BlockSpec((tm, DIL_W), lambda n, m: (ptile(m), 0)),
                  pl.BlockSpec((ns, D), lambda n, m: (0, 0)),
                  pl.BlockSpec((ns, DIL_W), lambda n, m: (0, 0)),
                  pl.BlockSpec((None, D, tn), lambda n, m: (layer, 0, n)),
                  pl.BlockSpec((None, DIL_W, tn), lambda n, m: (layer, 0, n)),
                  pl.BlockSpec((tm, tn), lambda n, m: (m, OFF_GA // tn + n)),
                  pl.BlockSpec((tm, tn), lambda n, m: (m, OFF_GB // tn + n))],
        out_specs=pl.BlockSpec((tm, tn), lambda n, m: (m, n)),
        out_shape=jax.ShapeDtypeStruct((nt, D), BF16),
        scratch_shapes=[pltpu.VMEM((D, tn), BF16), pltpu.VMEM((DIL_W, tn), BF16)],
        compiler_params=_cp(2), name="mixer_merge",
    )(ya_p, yb_p, ya_s, yb_s, w_out_ret, w_out_dil, proj, proj)


def _out_res_body(x_ref, mg_ref, w_ref, gp_ref, gs_ref, o_ref, w_bf, *, ns):
    @pl.when(pl.program_id(1) == 0)
    def _():
        w_bf[...] = w_ref[...].astype(BF16)

    def prompt():
        mix = jnp.dot(mg_ref[...], w_bf[...], preferred_element_type=F32)
        o_ref[...] = x_ref[...] + gp_ref[...] * mix

    def sample():
        mix = jnp.dot(mg_ref[:ns], w_bf[...], preferred_element_type=F32)
        o_ref[:ns] = x_ref[:ns] + gs_ref[...] * mix
        o_ref[ns:] = jnp.zeros_like(o_ref[ns:])

    _prompt_or_sample(pl.program_id(1), pl.num_programs(1), prompt, sample)


def _out_res(x_all, merged, w_out, mods, layer, np_rows, ns):
    nt = x_all.shape[0]
    tm = 1024
    tn = 512
    gp, gs = mods.specs(layer, 2, tm, tn, col_axis=True)
    return pl.pallas_call(
        functools.partial(_out_res_body, ns=ns),
        grid=(D // tn, np_rows // tm + 1),
        in_specs=[pl.BlockSpec((tm, tn), lambda n, m: (m, n)),
                  pl.BlockSpec((tm, D), lambda n, m: (m, 0)),
                  pl.BlockSpec((None, D, tn), lambda n, m: (layer, 0, n)),
                  gp, gs],
        out_specs=pl.BlockSpec((tm, tn), lambda n, m: (m, n)),
        out_shape=jax.ShapeDtypeStruct((nt, D), F32),
        scratch_shapes=[pltpu.VMEM((D, tn), BF16)],
        compiler_params=_cp(2), name="out_proj",
    )(x_all, merged, w_out, *mods.args)


def _pack_halves(h):
    a = lax.bitcast_convert_type(h[:, :D // 2].astype(BF16).astype(F32), U32)
    b = lax.bitcast_convert_type(h[:, D // 2:].astype(BF16).astype(F32), U32)
    return a | (b >> 16)


def _unpack_halves(w):
    a = lax.bitcast_convert_type(w & jnp.uint32(0xFFFF0000), F32)
    b = lax.bitcast_convert_type(w << 16, F32)
    return jnp.concatenate([a.astype(BF16), b.astype(BF16)], axis=1)


def _route_rows(x, g, sh, sc, rw, rb, cnt):
    tr = x.shape[0]
    h = _normed(x, g) * (1.0 + sc) + sh
    logits = jnp.dot(h, rw, precision=lax.Precision.HIGHEST, preferred_element_type=F32) + rb
    lane = lax.broadcasted_iota(I32, logits.shape, 1)
    work = logits
    tops = []
    member = jnp.zeros(logits.shape, F32)
    for _ in range(TOPK):
        m = jnp.max(work, axis=-1, keepdims=True)
        sel = jnp.min(jnp.where(work == m, lane, logits.shape[1]), axis=-1, keepdims=True)
        hit = lane == sel
        work = jnp.where(hit, -jnp.inf, work)
        member = jnp.where(hit, 1.0, member)
        tops.append((m, sel, hit))
    ri = lax.broadcasted_iota(I32, (tr, tr), 0)
    ci = lax.broadcasted_iota(I32, (tr, tr), 1)
    tri = jnp.where(ci < ri, 1.0, 0.0).astype(BF16)
    before = jnp.dot(tri, member.astype(BF16), preferred_element_type=F32) + cnt
    es = [jnp.exp(m - tops[0][0]) for m, _, _ in tops]
    den = es[0]
    for e in es[1:]:
        den = den + e
    choices = []
    for r, (m, sel, hit) in enumerate(tops):
        rank = jnp.sum(jnp.where(hit, before, 0.0), axis=-1, keepdims=True).astype(I32)
        choices.append((sel, es[r] / den, rank))
    return _pack_halves(h), choices, jnp.sum(member, axis=0, keepdims=True)


def _route_body(x_ref, g_ref, shp_ref, scp_ref, shs_ref, scs_ref, rw_ref, rb_ref,
                hp_ref, idx_ref, wt_ref, rank_ref, cnt_ref, *, ns):
    @pl.when(pl.program_id(0) == 0)
    def _():
        cnt_ref[...] = jnp.zeros_like(cnt_ref)

    def emit(rows, x, sh, sc):
        hp, choices, tile_cnt = _route_rows(x, g_ref[...], sh, sc, rw_ref[...], rb_ref[...], cnt_ref[...])
        cnt_ref[...] = cnt_ref[...] + tile_cnt
        hp_ref[:rows] = hp
        for r, (sel, wt, rank) in enumerate(choices):
            idx_ref[:rows, r:r + 1] = sel
            wt_ref[:rows, r:r + 1] = wt
            rank_ref[:rows, r:r + 1] = rank

    def prompt():
        emit(x_ref.shape[0], x_ref[...], shp_ref[...], scp_ref[...])

    def sample():
        emit(ns, x_ref[:ns], shs_ref[...], scs_ref[...])
        hp_ref[ns:] = jnp.zeros_like(hp_ref[ns:])
        idx_ref[ns:] = jnp.zeros_like(idx_ref[ns:])
        wt_ref[ns:] = jnp.zeros_like(wt_ref[ns:])
        rank_ref[ns:] = jnp.zeros_like(rank_ref[ns:])

    _prompt_or_sample(pl.program_id(0), pl.num_programs(0), prompt, sample)


def _route(x_all, g, mods, layer, router_w, router_b, np_rows, ns):
    nt = x_all.shape[0]
    tr = 512
    lanes = 128
    rw = jnp.zeros((D, lanes), F32).at[:, :N_EXP].set(router_w[layer])
    rb = jnp.full((1, lanes), -jnp.inf, F32).at[0, :N_EXP].set(router_b[layer])
    shp, shs = mods.specs(layer, 0, tr)
    scp, scs = mods.specs(layer, 1, tr)
    full = lambda shp_: pl.BlockSpec(shp_, lambda i: (0,) * len(shp_))
    rows = lambda w: pl.BlockSpec((tr, w), lambda i: (i, 0))
    return pl.pallas_call(
        functools.partial(_route_body, ns=ns),
        grid=(np_rows // tr + 1,),
        in_specs=[rows(D), pl.BlockSpec((None, 1, D), lambda i: (layer, 0, 0)),
                  shp, scp, shs, scs, full((D, lanes)), full((1, lanes))],
        out_specs=[rows(D // 2), rows(TOPK), rows(TOPK), rows(TOPK), full((1, lanes))],
        out_shape=[jax.ShapeDtypeStruct((nt, D // 2), U32), jax.ShapeDtypeStruct((nt, TOPK), I32),
                   jax.ShapeDtypeStruct((nt, TOPK), F32), jax.ShapeDtypeStruct((nt, TOPK), I32),
                   jax.ShapeDtypeStruct((1, lanes), F32)],
        compiler_params=_cp(1), name="norm2_route",
    )(x_all, g, mods.m4, mods.m4, mods.m3, mods.m3, rw, rb)


def _dispatch_body(last_ref, nused_ref, dest_hbm, hp_ref, xs_hbm, dsm, zeros_ref, sem_s, sem_z, sem_r,
                   *, nt, ntiles):
    i = pl.program_id(0)

    def zero_tile(t):
        return pltpu.make_async_copy(zeros_ref, xs_hbm.at[pl.ds(t * MOE_TM, MOE_TM), :], sem_z)

    @pl.when(i == 0)
    def _():
        zeros_ref[...] = jnp.zeros_like(zeros_ref)
        for e in range(N_EXP):
            @pl.when(last_ref[e] >= 0)
            def _():
                zero_tile(last_ref[e]).start()

        def fill(t, c):
            zero_tile(t).start()
            return c

        lax.fori_loop(nused_ref[0], ntiles, fill, 0)
        for e in range(N_EXP):
            @pl.when(last_ref[e] >= 0)
            def _():
                zero_tile(0).wait()

        def fill_wait(t, c):
            zero_tile(0).wait()
            return c

        lax.fori_loop(nused_ref[0], ntiles, fill_wait, 0)

    cp = pltpu.make_async_copy(dest_hbm.at[pl.ds(i * ROUTE_CHUNK, ROUTE_CHUNK)], dsm, sem_s)
    cp.start()
    cp.wait()
    rows = jnp.minimum(TOK_CHUNK, nt - i * TOK_CHUNK)

    def row_copy(r, d):
        return pltpu.make_async_copy(hp_ref.at[pl.ds(r, 1), :], xs_hbm.at[pl.ds(d, 1), :], sem_r)

    def issue(r, c):
        for k in range(TOPK):
            row_copy(r, dsm[r * TOPK + k]).start()
        return c

    lax.fori_loop(0, rows, issue, 0)

    def drain(r, c):
        for k in range(TOPK):
            row_copy(0, 0).wait()
        return c

    lax.fori_loop(0, rows, drain, 0)


def _dispatch(hp, dest_pad, last_tile, nused, ntiles):
    nt = hp.shape[0]
    steps = dest_pad.shape[0] // ROUTE_CHUNK
    return pl.pallas_call(
        functools.partial(_dispatch_body, nt=nt, ntiles=ntiles),
        grid_spec=pltpu.PrefetchScalarGridSpec(
            num_scalar_prefetch=2,
            grid=(steps,),
            in_specs=[pl.BlockSpec(memory_space=pl.ANY),
                      pl.BlockSpec((TOK_CHUNK, D // 2), lambda i, last, nu: (i, 0))],
            out_specs=pl.BlockSpec(memory_space=pl.ANY),
            scratch_shapes=[pltpu.SMEM((ROUTE_CHUNK,), I32),
                            pltpu.VMEM((MOE_TM, D // 2), U32),
                            pltpu.SemaphoreType.DMA(()), pltpu.SemaphoreType.DMA(()),
                            pltpu.SemaphoreType.DMA(())]),
        out_shape=jax.ShapeDtypeStruct((ntiles * MOE_TM, D // 2), U32),
        compiler_params=_cp(1), name="moe_dispatch",
    )(last_tile, nused, dest_pad, hp)


def _expert_changed(texp_ref, t):
    prev = texp_ref[jnp.maximum(t - 1, 0)]
    return (t == 0) | (texp_ref[t] != prev)


def _gate_up_body(texp_ref, nused_ref, x_ref, wg_ref, wu_ref, bg_ref, bu_ref, act_ref, wg_bf, wu_bf):
    t = pl.program_id(1)

    @pl.when(t < nused_ref[0])
    def _():
        @pl.when(_expert_changed(texp_ref, t))
        def _():
            wg_bf[...] = wg_ref[...].astype(BF16)
            wu_bf[...] = wu_ref[...].astype(BF16)

        x = _unpack_halves(x_ref[...])
        gate = jnp.dot(x, wg_bf[...], preferred_element_type=F32) + bg_ref[...]
        up = jnp.dot(x, wu_bf[...], preferred_element_type=F32) + bu_ref[...]
        gate = jnp.minimum(gate, SW_LIMIT)
        up = jnp.clip(up, -SW_LIMIT, SW_LIMIT)
        act_ref[...] = ((up + 1.0) * gate * jax.nn.sigmoid(SW_ALPHA * gate)).astype(act_ref.dtype)

    @pl.when(t >= nused_ref[0])
    def _():
        act_ref[...] = jnp.zeros_like(act_ref)


def _gate_up(xs, texp, nused, w_gate_up, b_gate_up, layer):
    p_rows = xs.shape[0]
    ntiles = p_rows // MOE_TM
    tf = 512
    nf = DFF // tf
    tile = lambda t, nu: jnp.minimum(t, nu[0] - 1)
    bgu = b_gate_up.reshape(N_LAYERS, N_EXP, 1, 2 * DFF)
    wspec = lambda off: pl.BlockSpec((None, None, D, tf),
                                     lambda f, t, te, nu: (layer, te[tile(t, nu)], 0, off + f))
    bspec = lambda off: pl.BlockSpec((None, None, 1, tf),
                                     lambda f, t, te, nu: (layer, te[tile(t, nu)], 0, off + f))
    return pl.pallas_call(
        _gate_up_body,
        grid_spec=pltpu.PrefetchScalarGridSpec(
            num_scalar_prefetch=2,
            grid=(nf, ntiles),
            in_specs=[pl.BlockSpec((MOE_TM, D // 2), lambda f, t, te, nu: (tile(t, nu), 0)),
                      wspec(0), wspec(nf), bspec(0), bspec(nf)],
            out_specs=pl.BlockSpec((MOE_TM, tf), lambda f, t, te, nu: (t, f)),
            scratch_shapes=[pltpu.VMEM((D, tf), BF16), pltpu.VMEM((D, tf), BF16)]),
        out_shape=jax.ShapeDtypeStruct((p_rows, DFF), BF16),
        compiler_params=_cp(2), name="moe_gate_up",
    )(texp, nused, xs, w_gate_up, w_gate_up, bgu, bgu)


def _down_body(texp_ref, nused_ref, a_ref, w_ref, b_ref, y_ref, w_bf):
    t = pl.program_id(1)

    @pl.when(t < nused_ref[0])
    def _():
        @pl.when(_expert_changed(texp_ref, t))
        def _():
            w_bf[...] = w_ref[...].astype(BF16)

        y_ref[...] = jnp.dot(a_ref[...], w_bf[...], preferred_element_type=F32) + b_ref[...]

    @pl.when(t >= nused_ref[0])
    def _():
        y_ref[...] = jnp.zeros_like(y_ref)


def _down(act, texp, nused, w_down, b_down, layer):
    p_rows = act.shape[0]
    ntiles = p_rows // MOE_TM
    tn = 1024
    tile = lambda t, nu: jnp.minimum(t, nu[0] - 1)
    bd = b_down.reshape(N_LAYERS, N_EXP, 1, D)
    return pl.pallas_call(
        _down_body,
        grid_spec=pltpu.PrefetchScalarGridSpec(
            num_scalar_prefetch=2,
            grid=(D // tn, ntiles),
            in_specs=[pl.BlockSpec((MOE_TM, DFF), lambda n, t, te, nu: (tile(t, nu), 0)),
                      pl.BlockSpec((None, None, DFF, tn), lambda n, t, te, nu: (layer, te[tile(t, nu)], 0, n)),
                      pl.BlockSpec((None, None, 1, tn), lambda n, t, te, nu: (layer, te[tile(t, nu)], 0, n))],
            out_specs=pl.BlockSpec((MOE_TM, tn), lambda n, t, te, nu: (t, n)),
            scratch_shapes=[pltpu.VMEM((DFF, tn), BF16)]),
        out_shape=jax.ShapeDtypeStruct((p_rows, D), F32),
        compiler_params=_cp(2), name="moe_down",
    )(texp, nused, act, w_down, bd)


def _combine_body(dest_hbm, ys_hbm, x_ref, wt_ref, gp_ref, gs_ref, o_ref, dsm, buf, sem_s, sem_r, *, ns):
    i = pl.program_id(0)
    cp = pltpu.make_async_copy(dest_hbm.at[pl.ds(i * ROUTE_CHUNK, ROUTE_CHUNK)], dsm, sem_s)
    cp.start()
    cp.wait()

    def row_copy(r, k, d):
        return pltpu.make_async_copy(ys_hbm.at[pl.ds(d, 1), :], buf.at[k, pl.ds(r, 1), :], sem_r)

    def gather(rows):
        def issue(r, c):
            for k in range(TOPK):
                row_copy(r, k, dsm[r * TOPK + k]).start()
            return c

        lax.fori_loop(0, rows, issue, 0)

        def drain(r, c):
            for k in range(TOPK):
                row_copy(0, 0, 0).wait()
            return c

        lax.fori_loop(0, rows, drain, 0)
        wt = wt_ref[:rows]
        moe = buf[0, :rows] * wt[:, 0:1]
        for k in range(1, TOPK):
            moe = moe + buf[k, :rows] * wt[:, k:k + 1]
        return moe

    def prompt():
        o_ref[...] = x_ref[...] + gp_ref[...] * gather(x_ref.shape[0])

    def sample():
        o_ref[:ns] = x_ref[:ns] + gs_ref[...] * gather(ns)
        o_ref[ns:] = jnp.zeros_like(o_ref[ns:])

    _prompt_or_sample(i, pl.num_programs(0), prompt, sample)


def _combine(x_all, ys, dest_pad, wts, mods, layer, np_rows, ns):
    nt = x_all.shape[0]
    assert np_rows % TOK_CHUNK == 0 and ns <= TOK_CHUNK
    any_spec = pl.BlockSpec(memory_space=pl.ANY)
    gp, gs = mods.specs(layer, 2, TOK_CHUNK)
    return pl.pallas_call(
        functools.partial(_combine_body, ns=ns),
        grid=(np_rows // TOK_CHUNK + 1,),
        in_specs=[any_spec, any_spec,
                  pl.BlockSpec((TOK_CHUNK, D), lambda i: (i, 0)),
                  pl.BlockSpec((TOK_CHUNK, TOPK), lambda i: (i, 0)),
                  gp, gs],
        out_specs=pl.BlockSpec((TOK_CHUNK, D), lambda i: (i, 0)),
        out_shape=jax.ShapeDtypeStruct((nt, D), F32),
        scratch_shapes=[pltpu.SMEM((ROUTE_CHUNK,), I32), pltpu.VMEM((TOPK, TOK_CHUNK, D), F32),
                        pltpu.SemaphoreType.DMA(()), pltpu.SemaphoreType.DMA(())],
        compiler_params=_cp(1), name="moe_combine",
    )(dest_pad, ys, x_all, wts, *mods.args)


def _moe(x_all, g, mods, layer, router_w, router_b, w_gate_up, b_gate_up, w_down, b_down, np_rows, ns):
    nt = x_all.shape[0]
    hp, idx, wts, rank, cnt = _route(x_all, g, mods, layer, router_w, router_b, np_rows, ns)
    counts = cnt[0, :N_EXP].astype(I32)
    padded = ((counts + MOE_TM - 1) // MOE_TM) * MOE_TM
    ends = jnp.cumsum(padded)
    gstart = ends - padded
    dest = gstart[idx] + rank
    n_entries = nt * TOPK
    n_chunks = -(-n_entries // ROUTE_CHUNK)
    dest_pad = jnp.zeros((n_chunks * ROUTE_CHUNK,), I32).at[:n_entries].set(dest.reshape(-1))
    ntiles = -(-n_entries // MOE_TM) + N_EXP
    tile_ends = ends // MOE_TM
    texp = jnp.minimum(jnp.searchsorted(tile_ends, jnp.arange(ntiles, dtype=I32), side='right'),
                       N_EXP - 1).astype(I32)
    nused = tile_ends[-1:].astype(I32)
    last_tile = jnp.where(padded > 0, tile_ends - 1, -1).astype(I32)
    xs = _dispatch(hp, dest_pad, last_tile, nused, ntiles)
    act = _gate_up(xs, texp, nused, w_gate_up, b_gate_up, layer)
    ys = _down(act, texp, nused, w_down, b_down, layer)
    return _combine(x_all, ys, dest_pad, wts, mods, layer, np_rows, ns)


def _final_norm_body(x_ref, g_ref, o_ref):
    o_ref[...] = _normed(x_ref[...], g_ref[...])


def _final_norm(x_all, g, np_rows, ns):
    tr = 1024
    gspec = pl.BlockSpec((1, D), lambda i: (0, 0))
    yp = pl.pallas_call(
        _final_norm_body, grid=(np_rows // tr,),
        in_specs=[pl.BlockSpec((tr, D), lambda i: (i, 0)), gspec],
        out_specs=pl.BlockSpec((tr, D), lambda i: (i, 0)),
        out_shape=jax.ShapeDtypeStruct((np_rows, D), F32),
        compiler_params=_cp(1), name="final_norm_prompt",
    )(x_all, g)
    sblk = np_rows // ns
    ysm = pl.pallas_call(
        _final_norm_body, grid=(1,),
        in_specs=[pl.BlockSpec((ns, D), lambda i: (sblk, 0)), gspec],
        out_specs=pl.BlockSpec((ns, D), lambda i: (0, 0)),
        out_shape=jax.ShapeDtypeStruct((ns, D), F32),
        compiler_params=_cp(1), name="final_norm_sample",
    )(x_all, g)
    return yp, ysm


def kernel(x_prompt, x_sample, state_ret, cache_kv_w128_d1, cache_kv_w512_d4, cache_kv_w2048_d16, c_prompt, c_sample, norm1_g, ada1_w, ada1_b, w_in, w_out_ret, w_out_dil, w_out, norm2_g, ada2_w, ada2_b, router_w, router_b, w_gate_up, b_gate_up, w_down, b_down, final_norm_g):
    nb, seq, _ = x_prompt.shape
    ns, dec_seq, _ = x_sample.shape
    assert dec_seq == 1 and seq == 2048 and ns % 16 == 0
    past = cache_kv_w2048_d16.shape[2]
    np_rows = nb * seq
    assert np_rows % ns == 0
    caches = (cache_kv_w128_d1, cache_kv_w512_d4, cache_kv_w2048_d16)

    x_all = jnp.concatenate([x_prompt.reshape(np_rows, D), x_sample.reshape(ns, D)], axis=0)
    nc = -(-(ns + nb) // 16) * 16
    c_all = jnp.zeros((nc, D), F32).at[:ns].set(c_sample).at[ns:ns + nb].set(c_prompt)
    mods1 = _Mods(_mods(c_all, ada1_w, ada1_b), ns, nb, seq)
    mods2 = _Mods(_mods(c_all, ada2_w, ada2_b), ns, nb, seq)
    g1 = norm1_g.reshape(N_LAYERS, 1, D)
    g2 = norm2_g.reshape(N_LAYERS, 1, D)

    ret_p, ret_s, kv_p, kv_s = [], [], [[] for _ in DIL], [[] for _ in DIL]
    for layer in range(N_LAYERS):
        h = _norm_mod(x_all, g1, mods1, layer, np_rows, ns)
        proj = _proj(h, w_in, layer)
        ya_p, st_p = _ret_prompt(proj, nb, seq)
        ya_s, st_s = _ret_sample(proj, state_ret, layer, np_rows, ns, past)
        yb_p = _dil_prompt(proj, nb, seq)
        yb_s = _dil_sample(proj, caches, layer, np_rows, ns)
        merged = _merge(ya_p, yb_p, ya_s, yb_s, proj, w_out_ret, w_out_dil, layer)
        x_all = _out_res(x_all, merged, w_out, mods1, layer, np_rows, ns)
        x_all = _moe(x_all, g2, mods2, layer, router_w, router_b, w_gate_up, b_gate_up, w_down, b_down,
                     np_rows, ns)
        ret_p.append(st_p)
        ret_s.append(st_s)
        for g, (window, _) in enumerate(DIL):
            c0 = OFF_DIL + g * 3 * DIL_W + DIL_W
            kvp = proj[:np_rows, c0:c0 + 2 * DIL_W].reshape(nb, seq, 2, DIL_H, DIL_E)
            kv_p[g].append(kvp[:, seq - min(window, seq):])
            kv_s[g].append(proj[np_rows:, c0:c0 + 2 * DIL_W].reshape(ns, 1, 2, DIL_H, DIL_E))
    y_p, y_s = _final_norm(x_all, final_norm_g.reshape(1, D), np_rows, ns)
    return (y_p.reshape(nb, seq, D), y_s.reshape(ns, 1, D),
            jnp.stack(ret_p), jnp.stack(kv_p[0]), jnp.stack(kv_p[1]), jnp.stack(kv_p[2]),
            jnp.stack(ret_s), jnp.stack(kv_s[0]), jnp.stack(kv_s[1]), jnp.stack(kv_s[2]))
```

```python
import functools

import jax
import jax.numpy as jnp
from jax import lax
from jax.experimental import pallas as pl
from jax.experimental.pallas import tpu as pltpu

F32 = jnp.float32
BF16 = jnp.bfloat16
U32 = jnp.uint32
I32 = jnp.int32

D = 2048
N_LAYERS = 2
RET_H = 8
RET_DK = 128
RET_DV = 256
RET_C = 128
ROPE_BASE = 10000.0
DIL = ((128, 1), (512, 4), (2048, 16))
DIL_H = 4
DIL_E = 128
DIL_W = DIL_H * DIL_E
N_EXP = 32
TOPK = 4
DFF = 2048
SW_LIMIT = 7.0
SW_ALPHA = 1.702
EPS = 1e-6

OFF_Q = 0
OFF_K = 1024
OFF_V = 2048
OFF_G = 4096
OFF_DIL = 6144
OFF_GA = 10752
OFF_GB = 12800
IN_W = 14848

VMEM_LIMIT = 56 * 1024 * 1024
MOE_TM = 512
ROUTE_CHUNK = 1024
TOK_CHUNK = ROUTE_CHUNK // TOPK


def _cp(n_grid_dims):
    return pltpu.CompilerParams(
        dimension_semantics=("arbitrary",) * n_grid_dims,
        vmem_limit_bytes=VMEM_LIMIT)


def _row_tile(n, cap):
    for k in range(1, n + 1):
        if n % k == 0 and n // k <= cap and (n // k) % 16 == 0:
            return n // k
    raise ValueError((n, cap))


def _silu(x):
    return x * jax.nn.sigmoid(x)


def _mods_body(c_ref, w_ref, b_ref, o_ref):
    a = _silu(c_ref[...]).astype(BF16)
    o_ref[...] = jnp.dot(a, w_ref[...].astype(BF16), preferred_element_type=F32) + b_ref[...]


def _mods(c_all, ada_w, ada_b):
    nc = c_all.shape[0]
    tn = 1024
    return pl.pallas_call(
        _mods_body,
        grid=(N_LAYERS, 3 * D // tn),
        in_specs=[pl.BlockSpec((nc, D), lambda l, n: (0, 0)),
                  pl.BlockSpec((None, D, tn), lambda l, n: (l, 0, n)),
                  pl.BlockSpec((None, 1, tn), lambda l, n: (l, 0, n))],
        out_specs=pl.BlockSpec((None, nc, tn), lambda l, n: (l, 0, n)),
        out_shape=jax.ShapeDtypeStruct((N_LAYERS, nc, 3 * D), F32),
        compiler_params=_cp(2), name="ada_mods",
    )(c_all, ada_w, ada_b.reshape(N_LAYERS, 1, 3 * D))


class _Mods:
    def __init__(self, m, ns, nb, seq):
        self.m3 = m
        self.m4 = m.reshape(m.shape[0], m.shape[1], 1, 3 * D)
        self.ns, self.nb, self.seq = ns, nb, seq

    @property
    def args(self):
        return (self.m4, self.m3)

    def specs(self, layer, which, tr, tn=D, col_axis=False):
        per = self.seq // tr
        ns, nb = self.ns, self.nb
        nblk = D // tn
        if col_axis:
            p_map = lambda n, i: (layer, ns + jnp.minimum(i // per, nb - 1), 0, which * nblk + n)
            s_map = lambda n, i: (layer, 0, which * nblk + n)
        else:
            p_map = lambda i: (layer, ns + jnp.minimum(i // per, nb - 1), 0, which * nblk)
            s_map = lambda i: (layer, 0, which * nblk)
        return [pl.BlockSpec((None, None, 1, tn), p_map), pl.BlockSpec((None, ns, tn), s_map)]


def _prompt_or_sample(step, n_steps, prompt_fn, sample_fn):
    pl.when(step < n_steps - 1)(prompt_fn)
    pl.when(step == n_steps - 1)(sample_fn)


def _normed(x, g):
    return x * lax.rsqrt(jnp.mean(x * x, axis=-1, keepdims=True) + EPS) * g


def _norm_mod_body(x_ref, g_ref, shp_ref, scp_ref, shs_ref, scs_ref, o_ref, *, ns):
    def prompt():
        y = _normed(x_ref[...], g_ref[...])
        o_ref[...] = (y * (1.0 + scp_ref[...]) + shp_ref[...]).astype(o_ref.dtype)

    def sample():
        y = _normed(x_ref[:ns], g_ref[...])
        o_ref[:ns] = (y * (1.0 + scs_ref[...]) + shs_ref[...]).astype(o_ref.dtype)
        o_ref[ns:] = jnp.zeros_like(o_ref[ns:])

    _prompt_or_sample(pl.program_id(0), pl.num_programs(0), prompt, sample)


def _norm_mod(x_all, g, mods, layer, np_rows, ns):
    nt = x_all.shape[0]
    tr = 1024
    shp, shs = mods.specs(layer, 0, tr)
    scp, scs = mods.specs(layer, 1, tr)
    return pl.pallas_call(
        functools.partial(_norm_mod_body, ns=ns),
        grid=(np_rows // tr + 1,),
        in_specs=[pl.BlockSpec((tr, D), lambda i: (i, 0)),
                  pl.BlockSpec((None, 1, D), lambda i: (layer, 0, 0)),
                  shp, scp, shs, scs],
        out_specs=pl.BlockSpec((tr, D), lambda i: (i, 0)),
        out_shape=jax.ShapeDtypeStruct((nt, D), BF16),
        compiler_params=_cp(1), name="norm1",
    )(x_all, g, mods.m4, mods.m4, mods.m3, mods.m3)


def _proj_body(x_ref, w_ref, o_ref, wbf_ref):
    @pl.when(pl.program_id(1) == 0)
    def _():
        wbf_ref[...] = w_ref[...].astype(BF16)

    o_ref[...] = jnp.dot(x_ref[...], wbf_ref[...], preferred_element_type=F32).astype(o_ref.dtype)


def _proj(h, w_in, layer):
    nt = h.shape[0]
    tm = _row_tile(nt, 1376)
    tn = 1024
    return pl.pallas_call(
        _proj_body,
        grid=(pl.cdiv(IN_W, tn), nt // tm),
        in_specs=[pl.BlockSpec((tm, D), lambda n, m: (m, 0)),
                  pl.BlockSpec((None, D, tn), lambda n, m: (layer, 0, n))],
        out_specs=pl.BlockSpec((tm, tn), lambda n, m: (m, n)),
        out_shape=jax.ShapeDtypeStruct((nt, IN_W), F32),
        scratch_shapes=[pltpu.VMEM((D, tn), BF16)],
        compiler_params=_cp(2), name="in_proj",
    )(h, w_in)


def _rope_tables(pos):
    half = RET_DK // 2
    inv = ROPE_BASE ** -jnp.linspace(0.0, 1.0, half, dtype=F32)
    ang = pos.astype(F32)[:, None] * inv[None, :]
    cos = jnp.cos(ang)
    sin = jnp.sin(ang)
    return jnp.concatenate([cos, cos], axis=-1), jnp.concatenate([-sin, sin], axis=-1)


def _decay_tables(chunk):
    log_g = jnp.log1p(-jnp.exp2(-5.0 - jnp.arange(RET_H, dtype=F32)))
    idx = jnp.arange(chunk, dtype=F32)
    rel = idx[:, None] - idx[None, :]
    inner = jnp.where(rel >= 0, jnp.exp(log_g[:, None, None] * jnp.maximum(rel, 0.0)), 0.0)
    q_decay = jnp.exp(log_g[:, None] * (idx + 1.0)[None, :])
    k_decay = jnp.exp(log_g[:, None] * (chunk - 1.0 - idx)[None, :])
    chunk_decay = jnp.exp(log_g * chunk)
    return inner, q_decay, k_decay, chunk_decay


def _rot(x, cos, sin):
    return x * cos + pltpu.roll(x, RET_DK // 2, 1) * sin


def _head_norm_gate(o, g):
    on = o * lax.rsqrt(jnp.mean(o * o, axis=-1, keepdims=True) + EPS)
    return _silu(g) * on


def _ret_prompt_body(q_ref, k_ref, v_ref, g_ref, cos_ref, sin_ref, inner_ref, qd_ref, kd_ref, cd_ref,
                     ya_ref, s_ref):
    @pl.when(pl.program_id(1) == 0)
    def _():
        s_ref[...] = jnp.zeros_like(s_ref)

    cos = cos_ref[...]
    sin = sin_ref[...]
    for h in range(RET_H):
        q = _rot(q_ref[:, h * RET_DK:(h + 1) * RET_DK], cos, sin)
        k = _rot(k_ref[:, h * RET_DK:(h + 1) * RET_DK], cos, sin) * (RET_DK ** -0.5)
        vb = v_ref[:, h * RET_DV:(h + 1) * RET_DV].astype(BF16)
        s = lax.dot_general(q.astype(BF16), k.astype(BF16), (((1,), (1,)), ((), ())),
                            preferred_element_type=F32) * inner_ref[h]
        state = s_ref[h]
        o = (jnp.dot(s.astype(BF16), vb, preferred_element_type=F32)
             + jnp.dot((q * qd_ref[h]).astype(BF16), state.astype(BF16), preferred_element_type=F32))
        kt = (k * kd_ref[h]).T.astype(BF16)
        s_ref[h] = state * cd_ref[h] + jnp.dot(kt, vb, preferred_element_type=F32)
        y = _head_norm_gate(o, g_ref[:, h * RET_DV:(h + 1) * RET_DV])
        ya_ref[:, h * RET_DV:(h + 1) * RET_DV] = y.astype(ya_ref.dtype)


def _ret_prompt(proj, nb, seq):
    nch = seq // RET_C
    cos, sin = _rope_tables(jnp.arange(seq, dtype=I32))
    inner, qd, kd, cd = _decay_tables(RET_C)
    qd_b = jnp.broadcast_to(qd[:, :, None], (RET_H, RET_C, RET_DK))
    kd_b = jnp.broadcast_to(kd[:, :, None], (RET_H, RET_C, RET_DK))
    cd_b = jnp.broadcast_to(cd[:, None, None], (RET_H, 1, RET_DV))
    row = lambda b, c: b * nch + c
    full3 = lambda shp: pl.BlockSpec(shp, lambda b, c: (0, 0, 0))
    return pl.pallas_call(
        _ret_prompt_body,
        grid=(nb, nch),
        in_specs=[pl.BlockSpec((RET_C, 1024), lambda b, c: (row(b, c), OFF_Q // 1024)),
                  pl.BlockSpec((RET_C, 1024), lambda b, c: (row(b, c), OFF_K // 1024)),
                  pl.BlockSpec((RET_C, 2048), lambda b, c: (row(b, c), OFF_V // 2048)),
                  pl.BlockSpec((RET_C, 2048), lambda b, c: (row(b, c), OFF_G // 2048)),
                  pl.BlockSpec((RET_C, RET_DK), lambda b, c: (c, 0)),
                  pl.BlockSpec((RET_C, RET_DK), lambda b, c: (c, 0)),
                  full3((RET_H, RET_C, RET_C)), full3((RET_H, RET_C, RET_DK)),
                  full3((RET_H, RET_C, RET_DK)), full3((RET_H, 1, RET_DV))],
        out_specs=[pl.BlockSpec((RET_C, D), lambda b, c: (row(b, c), 0)),
                   pl.BlockSpec((None, RET_H, RET_DK, RET_DV), lambda b, c: (b, 0, 0, 0))],
        out_shape=[jax.ShapeDtypeStruct((nb * seq, D), BF16),
                   jax.ShapeDtypeStruct((nb, RET_H, RET_DK, RET_DV), F32)],
        compiler_params=_cp(2), name="retention_prompt",
    )(proj, proj, proj, proj, cos, sin, inner, qd_b, kd_b, cd_b)


RS_NB = 8


def _ret_sample_body(q_ref, k_ref, v_ref, g_ref, cos_ref, sin_ref, inner_ref, qd_ref, kd_ref, cd_ref,
                     st_ref, ya_ref, so_ref):
    i = pl.program_id(0)
    cos = cos_ref[...]
    sin = sin_ref[...]
    row_i = lax.broadcasted_iota(I32, (RET_DK, RET_DK), 0)
    col_i = lax.broadcasted_iota(I32, (RET_DK, RET_DK), 1)
    eye = row_i == col_i
    rows = pl.ds(pl.multiple_of(i * RS_NB, RS_NB), RS_NB)
    q_all = q_ref[rows, :]
    k_all = k_ref[rows, :]
    v_all = v_ref[rows, :]
    g_all = g_ref[rows, :]
    out_rows = []
    for j in range(RS_NB):
        heads = []
        for h in range(RET_H):
            q = _rot(q_all[j:j + 1, h * RET_DK:(h + 1) * RET_DK], cos, sin)
            k = _rot(k_all[j:j + 1, h * RET_DK:(h + 1) * RET_DK], cos, sin) * (RET_DK ** -0.5)
            v = v_all[j:j + 1, h * RET_DV:(h + 1) * RET_DV]
            s = jnp.sum(q * k, axis=-1, keepdims=True) * inner_ref[h]
            state = st_ref[j, h]
            q8 = jnp.broadcast_to(q * qd_ref[h], (8, RET_DK)).astype(BF16)
            qs = jnp.dot(q8, state.astype(BF16), preferred_element_type=F32)[0:1]
            o = s * v + qs
            kdiag = jnp.where(eye, jnp.broadcast_to(k * kd_ref[h], (RET_DK, RET_DK)), 0.0).astype(BF16)
            vb = jnp.broadcast_to(v, (RET_DK, RET_DV)).astype(BF16)
            so_ref[j, h] = state * cd_ref[h] + jnp.dot(kdiag, vb, preferred_element_type=F32)
            heads.append(_head_norm_gate(o, g_all[j:j + 1, h * RET_DV:(h + 1) * RET_DV]))
        out_rows.append(jnp.concatenate(heads, axis=1))
    ya_ref[...] = jnp.concatenate(out_rows, axis=0)


def _ret_sample(proj, state_ret, layer, np_rows, ns, past):
    cos, sin = _rope_tables(past + jnp.arange(1, dtype=I32))
    inner, qd, kd, cd = _decay_tables(1)
    inner_b = inner.reshape(RET_H, 1, 1)
    qd_b = jnp.broadcast_to(qd[:, :, None], (RET_H, 1, RET_DK))
    kd_b = jnp.broadcast_to(kd[:, :, None], (RET_H, 1, RET_DK))
    cd_b = jnp.broadcast_to(cd[:, None, None], (RET_H, 1, RET_DV))
    sblk = np_rows // ns
    full = lambda shp: pl.BlockSpec(shp, lambda i: (0,) * len(shp))
    return pl.pallas_call(
        _ret_sample_body,
        grid=(ns // RS_NB,),
        in_specs=[pl.BlockSpec((ns, 1024), lambda i: (sblk, OFF_Q // 1024)),
                  pl.BlockSpec((ns, 1024), lambda i: (sblk, OFF_K // 1024)),
                  pl.BlockSpec((ns, 2048), lambda i: (sblk, OFF_V // 2048)),
                  pl.BlockSpec((ns, 2048), lambda i: (sblk, OFF_G // 2048)),
                  full((1, RET_DK)), full((1, RET_DK)),
                  full((RET_H, 1, 1)), full((RET_H, 1, RET_DK)), full((RET_H, 1, RET_DK)),
                  full((RET_H, 1, RET_DV)),
                  pl.BlockSpec((None, RS_NB, RET_H, RET_DK, RET_DV), lambda i: (layer, i, 0, 0, 0))],
        out_specs=[pl.BlockSpec((RS_NB, D), lambda i: (i, 0)),
                   pl.BlockSpec((RS_NB, RET_H, RET_DK, RET_DV), lambda i: (i, 0, 0, 0))],
        out_shape=[jax.ShapeDtypeStruct((ns, D), F32),
                   jax.ShapeDtypeStruct((ns, RET_H, RET_DK, RET_DV), F32)],
        compiler_params=_cp(1), name="retention_sample",
    )(proj, proj, proj, proj, cos, sin, inner_b, qd_b, kd_b, cd_b, state_ret)


def _merge_groups(parts):
    m_all = parts[0][1]
    for _, m_g, _ in parts[1:]:
        m_all = jnp.maximum(m_all, m_g)
    num = 0.0
    den = 0.0
    for n_g, m_g, l_g in parts:
        w = jnp.exp(m_g - m_all)
        num = num + n_g * w
        den = den + l_g * w
    return num / den


def _dil_prompt_body(*refs, seq):
    qkv = refs[:9]
    yb_ref = refs[9]
    num_ref, m_ref, l_ref = refs[10:13]
    band = 128
    scale = DIL_E ** -0.5
    qi = lax.broadcasted_iota(I32, (band, band), 0)
    kj1 = lax.broadcasted_iota(I32, (band, band), 1)
    qi2 = lax.broadcasted_iota(I32, (band, 2 * band), 0)
    kj2 = lax.broadcasted_iota(I32, (band, 2 * band), 1)
    first_valid = kj1 <= qi
    later_valid = (kj2 >= qi2) & (kj2 <= qi2 + band)
    for g, (window, dil) in enumerate(DIL):
        assert window // dil == band
        q_ref, k_ref, v_ref = qkv[3 * g:3 * g + 3]
        nblk = seq // dil // band
        for p in range(dil):
            for n in range(nblk):
                rows_q = pl.ds(p + n * band * dil, band, stride=dil)
                qb = q_ref[rows_q, :].astype(BF16)
                if n == 0:
                    rows_k = rows_q
                    valid = first_valid
                else:
                    rows_k = pl.ds(p + (n - 1) * band * dil, 2 * band, stride=dil)
                    valid = later_valid
                kk = k_ref[rows_k, :].astype(BF16)
                vv = v_ref[rows_k, :].astype(BF16)
                s = lax.dot_general(qb, kk, (((1,), (1,)), ((), ())), preferred_element_type=F32) * scale
                s = jnp.where(valid, s, -jnp.inf)
                m = jnp.max(s, axis=-1, keepdims=True)
                pe = jnp.exp(s - m)
                l = jnp.sum(pe, axis=-1, keepdims=True)
                num_ref[g, rows_q, :] = jnp.dot(pe.astype(BF16), vv, preferred_element_type=F32)
                m_ref[g, rows_q, :] = jnp.broadcast_to(m, (band, DIL_E))
                l_ref[g, rows_q, :] = jnp.broadcast_to(l, (band, DIL_E))
    parts = [(num_ref[g], m_ref[g], l_ref[g]) for g in range(len(DIL))]
    yb_ref[...] = _merge_groups(parts).astype(yb_ref.dtype)


def _dil_prompt(proj, nb, seq):
    in_specs = []
    for g in range(len(DIL)):
        for j in range(3):
            cb = (OFF_DIL + g * 3 * DIL_W + j * DIL_W) // DIL_E
            in_specs.append(pl.BlockSpec((seq, DIL_E), lambda b, h, cb=cb: (b, cb + h)))
    scr = pltpu.VMEM((len(DIL), seq, DIL_E), F32)
    return pl.pallas_call(
        functools.partial(_dil_prompt_body, seq=seq),
        grid=(nb, DIL_H),
        in_specs=in_specs,
        out_specs=pl.BlockSpec((seq, DIL_E), lambda b, h: (b, h)),
        out_shape=jax.ShapeDtypeStruct((nb * seq, DIL_W), BF16),
        scratch_shapes=[scr, scr, scr],
        compiler_params=_cp(2), name="dilated_prompt",
    )(*([proj] * 9))


DS_NB = 8


def _dil_sample_body(c0_ref, c1_ref, c2_ref, q0_ref, q1_ref, q2_ref, yb_ref):
    i = pl.program_id(0)
    caches = (c0_ref, c1_ref, c2_ref)
    scale = DIL_E ** -0.5
    rows = pl.ds(pl.multiple_of(i * DS_NB, DS_NB), DS_NB)
    news = [ref[rows, :] for ref in (q0_ref, q1_ref, q2_ref)]
    out_rows = []
    for j in range(DS_NB):
        parts = []
        for g in range(len(DIL)):
            new = news[g][j:j + 1]
            q4 = jnp.concatenate([new[:, h * DIL_E:(h + 1) * DIL_E] for h in range(DIL_H)], axis=0)
            k4 = jnp.concatenate([new[:, DIL_W + h * DIL_E:DIL_W + (h + 1) * DIL_E] for h in range(DIL_H)], axis=0)
            v4 = jnp.concatenate([new[:, 2 * DIL_W + h * DIL_E:2 * DIL_W + (h + 1) * DIL_E] for h in range(DIL_H)],
                                 axis=0)
            kc = caches[g][j, :, 0:DIL_H, :]
            vc = caches[g][j, :, DIL_H:2 * DIL_H, :]
            s_c = jnp.sum(kc * q4[None], axis=-1, keepdims=True) * scale
            s_n = jnp.sum(k4 * q4, axis=-1, keepdims=True) * scale
            m = jnp.maximum(jnp.max(s_c, axis=0), s_n)
            p_c = jnp.exp(s_c - m[None])
            p_n = jnp.exp(s_n - m)
            l = jnp.sum(p_c, axis=0) + p_n
            num = jnp.sum(p_c * vc, axis=0) + p_n * v4
            parts.append((num, m, l))
        y = _merge_groups(parts)
        out_rows.append(jnp.concatenate([y[h:h + 1] for h in range(DIL_H)], axis=1))
    yb_ref[...] = jnp.concatenate(out_rows, axis=0)


def _dil_sample(proj, caches, layer, np_rows, ns):
    sblk = np_rows // ns
    views = []
    in_specs = []
    for cache, (window, dil) in zip(caches, DIL):
        wb = cache.shape[2]
        assert wb == window and wb % dil == 0
        views.append(cache.reshape(cache.shape[0] * ns, wb // dil, dil, 2 * DIL_H, DIL_E))
        in_specs.append(pl.BlockSpec((DS_NB, wb // dil, None, 2 * DIL_H, DIL_E),
                                     lambda i: (layer * (ns // DS_NB) + i, 0, 0, 0, 0)))
    for g in range(len(DIL)):
        in_specs.append(pl.BlockSpec((ns, 3 * DIL_W), lambda i, g=g: (sblk, OFF_DIL // (3 * DIL_W) + g)))
    return pl.pallas_call(
        _dil_sample_body,
        grid=(ns // DS_NB,),
        in_specs=in_specs,
        out_specs=pl.BlockSpec((DS_NB, DIL_W), lambda i: (i, 0)),
        out_shape=jax.ShapeDtypeStruct((ns, DIL_W), F32),
        compiler_params=_cp(1), name="dilated_sample",
    )(*views, proj, proj, proj)


def _kv_out_body(*refs):
    o_ref = refs[-1]
    layer = pl.program_id(0)
    for l in range(N_LAYERS):
        @pl.when(layer == l)
        def _(k_ref=refs[2 * l], v_ref=refs[2 * l + 1]):
            for h in range(DIL_H):
                o_ref[:, 0, h, :] = k_ref[:, h * DIL_E:(h + 1) * DIL_E]
                o_ref[:, 1, h, :] = v_ref[:, h * DIL_E:(h + 1) * DIL_E]


def _kv_out(projs, g, nb, seq):
    window = min(DIL[g][0], seq)
    rows = min(window, 512)
    nr = window // rows
    cb = (OFF_DIL + g * 3 * DIL_W + DIL_W) // DIL_W
    first = (seq - window) // rows
    per_b = seq // rows
    last = (nb - 1) * per_b + first + nr - 1
    in_specs = []
    for l in range(N_LAYERS):
        def rblk(ll, b, r, l=l):
            cur = b * per_b + first + r
            return jnp.where(ll == l, cur, jnp.where(ll < l, first, last))
        in_specs.append(pl.BlockSpec((rows, DIL_W), lambda ll, b, r, f=rblk: (f(ll, b, r), cb)))
        in_specs.append(pl.BlockSpec((rows, DIL_W), lambda ll, b, r, f=rblk: (f(ll, b, r), cb + 1)))
    args = []
    for p in projs:
        args += [p, p]
    return pl.pallas_call(
        _kv_out_body,
        grid=(N_LAYERS, nb, nr),
        in_specs=in_specs,
        out_specs=pl.BlockSpec((None, None, rows, 2, DIL_H, DIL_E), lambda ll, b, r: (ll, b, r, 0, 0, 0)),
        out_shape=jax.ShapeDtypeStruct((N_LAYERS, nb, window, 2, DIL_H, DIL_E), F32),
        compiler_params=_cp(3), name="kv_prompt_out",
    )(*args)


def _merge_body(yap_ref, ybp_ref, yas_ref, ybs_ref, wr_ref, wd_ref, ga_ref, gb_ref, o_ref, wr_bf, wd_bf, *, ns):
    @pl.when(pl.program_id(1) == 0)
    def _():
        wr_bf[...] = wr_ref[...].astype(BF16)
        wd_bf[...] = wd_ref[...].astype(BF16)

    def mixed(ya, yb, ga, gb):
        y_a = jnp.dot(ya, wr_bf[...], preferred_element_type=F32)
        y_b = jnp.dot(yb, wd_bf[...], preferred_element_type=F32)
        return (jax.nn.sigmoid(ga) * y_a + jax.nn.sigmoid(gb) * y_b).astype(o_ref.dtype)

    def prompt():
        o_ref[...] = mixed(yap_ref[...], ybp_ref[...], ga_ref[...], gb_ref[...])

    def sample():
        o_ref[:ns] = mixed(yas_ref[...].astype(BF16), ybs_ref[...].astype(BF16), ga_ref[:ns], gb_ref[:ns])
        o_ref[ns:] = jnp.zeros_like(o_ref[ns:])

    _prompt_or_sample(pl.program_id(1), pl.num_programs(1), prompt, sample)


def _merge(ya_p, yb_p, ya_s, yb_s, proj, w_out_ret, w_out_dil, layer):
    nt = proj.shape[0]
    np_rows = ya_p.shape[0]
    ns = ya_s.shape[0]
    tm = 1024
    tn = 512
    npt = np_rows // tm
    ptile = lambda m: jnp.minimum(m, npt - 1)
    return pl.pallas_call(
        functools.partial(_merge_body, ns=ns),
        grid=(D // tn, npt + 1),
        in_specs=[pl.BlockSpec((tm, D), lambda n, m: (ptile(m), 0)),
                  pl.BlockSpec((tm, DIL_W), lambda n, m: (ptile(m), 0)),
                  pl.BlockSpec((ns, D), lambda n, m: (0, 0)),
                  pl.BlockSpec((ns, DIL_W), lambda n, m: (0, 0)),
                  pl.BlockSpec((None, D, tn), lambda n, m: (layer, 0, n)),
                  pl.BlockSpec((None, DIL_W, tn), lambda n, m: (layer, 0, n)),
                  pl.BlockSpec((tm, tn), lambda n, m: (m, OFF_GA // tn + n)),
                  pl.BlockSpec((tm, tn), lambda n, m: (m, OFF_GB // tn + n))],
        out_specs=pl.BlockSpec((tm, tn), lambda n, m: (m, n)),
        out_shape=jax.ShapeDtypeStruct((nt, D), BF16),
        scratch_shapes=[pltpu.VMEM((D, tn), BF16), pltpu.VMEM((DIL_W, tn), BF16)],
        compiler_params=_cp(2), name="mixer_merge",
    )(ya_p, yb_p, ya_s, yb_s, w_out_ret, w_out_dil, proj, proj)


def _out_res_body(x_ref, mg_ref, w_ref, gp_ref, gs_ref, o_ref, w_bf, *, ns):
    @pl.when(pl.program_id(1) == 0)
    def _():
        w_bf[...] = w_ref[...].astype(BF16)

    def prompt():
        mix = jnp.dot(mg_ref[...], w_bf[...], preferred_element_type=F32)
        o_ref[...] = x_ref[...] + gp_ref[...] * mix

    def sample():
        mix = jnp.dot(mg_ref[:ns], w_bf[...], preferred_element_type=F32)
        o_ref[:ns] = x_ref[:ns] + gs_ref[...] * mix
        o_ref[ns:] = jnp.zeros_like(o_ref[ns:])

    _prompt_or_sample(pl.program_id(1), pl.num_programs(1), prompt, sample)


def _out_res(x_all, merged, w_out, mods, layer, np_rows, ns):
    nt = x_all.shape[0]
    tm = 1024
    tn = 512
    gp, gs = mods.specs(layer, 2, tm, tn, col_axis=True)
    return pl.pallas_call(
        functools.partial(_out_res_body, ns=ns),
        grid=(D // tn, np_rows // tm + 1),
        in_specs=[pl.BlockSpec((tm, tn), lambda n, m: (m, n)),
                  pl.BlockSpec((tm, D), lambda n, m: (m, 0)),
                  pl.BlockSpec((None, D, tn), lambda n, m: (layer, 0, n)),
                  gp, gs],
        out_specs=pl.BlockSpec((tm, tn), lambda n, m: (m, n)),
        out_shape=jax.ShapeDtypeStruct((nt, D), F32),
        scratch_shapes=[pltpu.VMEM((D, tn), BF16)],
        compiler_params=_cp(2), name="out_proj",
    )(x_all, merged, w_out, *mods.args)


def _pack_halves(h):
    a = lax.bitcast_convert_type(h[:, :D // 2].astype(BF16).astype(F32), U32)
    b = lax.bitcast_convert_type(h[:, D // 2:].astype(BF16).astype(F32), U32)
    return a | (b >> 16)


def _unpack_halves(w):
    a = lax.bitcast_convert_type(w & jnp.uint32(0xFFFF0000), F32)
    b = lax.bitcast_convert_type(w << 16, F32)
    return jnp.concatenate([a.astype(BF16), b.astype(BF16)], axis=1)


def _route_rows(x, g, sh, sc, rw, rb, cnt):
    tr = x.shape[0]
    h = _normed(x, g) * (1.0 + sc) + sh
    logits = jnp.dot(h, rw, precision=lax.Precision.HIGHEST, preferred_element_type=F32) + rb
    lane = lax.broadcasted_iota(I32, logits.shape, 1)
    work = logits
    tops = []
    member = jnp.zeros(logits.shape, F32)
    for _ in range(TOPK):
        m = jnp.max(work, axis=-1, keepdims=True)
        sel = jnp.min(jnp.where(work == m, lane, logits.shape[1]), axis=-1, keepdims=True)
        hit = lane == sel
        work = jnp.where(hit, -jnp.inf, work)
        member = jnp.where(hit, 1.0, member)
        tops.append((m, sel, hit))
    ri = lax.broadcasted_iota(I32, (tr, tr), 0)
    ci = lax.broadcasted_iota(I32, (tr, tr), 1)
    tri = jnp.where(ci < ri, 1.0, 0.0).astype(BF16)
    before = jnp.dot(tri, member.astype(BF16), preferred_element_type=F32) + cnt
    es = [jnp.exp(m - tops[0][0]) for m, _, _ in tops]
    den = es[0]
    for e in es[1:]:
        den = den + e
    choices = []
    for r, (m, sel, hit) in enumerate(tops):
        rank = jnp.sum(jnp.where(hit, before, 0.0), axis=-1, keepdims=True).astype(I32)
        choices.append((sel, es[r] / den, rank))
    return _pack_halves(h), choices, jnp.sum(member, axis=0, keepdims=True)


def _route_body(x_ref, g_ref, shp_ref, scp_ref, shs_ref, scs_ref, rw_ref, rb_ref,
                hp_ref, idx_ref, wt_ref, rank_ref, cnt_ref, *, ns):
    @pl.when(pl.program_id(0) == 0)
    def _():
        cnt_ref[...] = jnp.zeros_like(cnt_ref)

    def emit(rows, x, sh, sc):
        hp, choices, tile_cnt = _route_rows(x, g_ref[...], sh, sc, rw_ref[...], rb_ref[...], cnt_ref[...])
        cnt_ref[...] = cnt_ref[...] + tile_cnt
        hp_ref[:rows] = hp
        for r, (sel, wt, rank) in enumerate(choices):
            idx_ref[:rows, r:r + 1] = sel
            wt_ref[:rows, r:r + 1] = wt
            rank_ref[:rows, r:r + 1] = rank

    def prompt():
        emit(x_ref.shape[0], x_ref[...], shp_ref[...], scp_ref[...])

    def sample():
        emit(ns, x_ref[:ns], shs_ref[...], scs_ref[...])
        hp_ref[ns:] = jnp.zeros_like(hp_ref[ns:])
        idx_ref[ns:] = jnp.zeros_like(idx_ref[ns:])
        wt_ref[ns:] = jnp.zeros_like(wt_ref[ns:])
        rank_ref[ns:] = jnp.zeros_like(rank_ref[ns:])

    _prompt_or_sample(pl.program_id(0), pl.num_programs(0), prompt, sample)


def _route(x_all, g, mods, layer, router_w, router_b, np_rows, ns):
    nt = x_all.shape[0]
    tr = 512
    lanes = 128
    rw = jnp.zeros((D, lanes), F32).at[:, :N_EXP].set(router_w[layer])
    rb = jnp.full((1, lanes), -jnp.inf, F32).at[0, :N_EXP].set(router_b[layer])
    shp, shs = mods.specs(layer, 0, tr)
    scp, scs = mods.specs(layer, 1, tr)
    full = lambda shp_: pl.BlockSpec(shp_, lambda i: (0,) * len(shp_))
    rows = lambda w: pl.BlockSpec((tr, w), lambda i: (i, 0))
    return pl.pallas_call(
        functools.partial(_route_body, ns=ns),
        grid=(np_rows // tr + 1,),
        in_specs=[rows(D), pl.BlockSpec((None, 1, D), lambda i: (layer, 0, 0)),
                  shp, scp, shs, scs, full((D, lanes)), full((1, lanes))],
        out_specs=[rows(D // 2), rows(TOPK), rows(TOPK), rows(TOPK), full((1, lanes))],
        out_shape=[jax.ShapeDtypeStruct((nt, D // 2), U32), jax.ShapeDtypeStruct((nt, TOPK), I32),
                   jax.ShapeDtypeStruct((nt, TOPK), F32), jax.ShapeDtypeStruct((nt, TOPK), I32),
                   jax.ShapeDtypeStruct((1, lanes), F32)],
        compiler_params=_cp(1), name="norm2_route",
    )(x_all, g, mods.m4, mods.m4, mods.m3, mods.m3, rw, rb)


def _dispatch_body(last_ref, nused_ref, dest_hbm, hp_ref, xs_hbm, dsm, zeros_ref, sem_s, sem_z, sem_r,
                   *, nt, ntiles):
    i = pl.program_id(0)

    def zero_tile(t):
        return pltpu.make_async_copy(zeros_ref, xs_hbm.at[pl.ds(t * MOE_TM, MOE_TM), :], sem_z)

    @pl.when(i == 0)
    def _():
        zeros_ref[...] = jnp.zeros_like(zeros_ref)
        for e in range(N_EXP):
            @pl.when(last_ref[e] >= 0)
            def _():
                zero_tile(last_ref[e]).start()

        def fill(t, c):
            zero_tile(t).start()
            return c

        lax.fori_loop(nused_ref[0], ntiles, fill, 0)
        for e in range(N_EXP):
            @pl.when(last_ref[e] >= 0)
            def _():
                zero_tile(0).wait()

        def fill_wait(t, c):
            zero_tile(0).wait()
            return c

        lax.fori_loop(nused_ref[0], ntiles, fill_wait, 0)

    cp = pltpu.make_async_copy(dest_hbm.at[pl.ds(i * ROUTE_CHUNK, ROUTE_CHUNK)], dsm, sem_s)
    cp.start()
    cp.wait()
    rows = jnp.minimum(TOK_CHUNK, nt - i * TOK_CHUNK)

    def row_copy(r, d):
        return pltpu.make_async_copy(hp_ref.at[pl.ds(r, 1), :], xs_hbm.at[pl.ds(d, 1), :], sem_r)

    def issue(r, c):
        for k in range(TOPK):
            row_copy(r, dsm[r * TOPK + k]).start()
        return c

    lax.fori_loop(0, rows, issue, 0)

    def drain(r, c):
        for k in range(TOPK):
            row_copy(0, 0).wait()
        return c

    lax.fori_loop(0, rows, drain, 0)


def _dispatch(hp, dest_pad, last_tile, nused, ntiles):
    nt = hp.shape[0]
    steps = dest_pad.shape[0] // ROUTE_CHUNK
    return pl.pallas_call(
        functools.partial(_dispatch_body, nt=nt, ntiles=ntiles),
        grid_spec=pltpu.PrefetchScalarGridSpec(
            num_scalar_prefetch=2,
            grid=(steps,),
            in_specs=[pl.BlockSpec(memory_space=pl.ANY),
                      pl.BlockSpec((TOK_CHUNK, D // 2), lambda i, last, nu: (i, 0))],
            out_specs=pl.BlockSpec(memory_space=pl.ANY),
            scratch_shapes=[pltpu.SMEM((ROUTE_CHUNK,), I32),
                            pltpu.VMEM((MOE_TM, D // 2), U32),
                            pltpu.SemaphoreType.DMA(()), pltpu.SemaphoreType.DMA(()),
                            pltpu.SemaphoreType.DMA(())]),
        out_shape=jax.ShapeDtypeStruct((ntiles * MOE_TM, D // 2), U32),
        compiler_params=_cp(1), name="moe_dispatch",
    )(last_tile, nused, dest_pad, hp)


def _expert_changed(texp_ref, t):
    prev = texp_ref[jnp.maximum(t - 1, 0)]
    return (t == 0) | (texp_ref[t] != prev)


def _gate_up_body(texp_ref, nused_ref, x_ref, wg_ref, wu_ref, bg_ref, bu_ref, act_ref, wg_bf, wu_bf):
    t = pl.program_id(1)

    @pl.when(t < nused_ref[0])
    def _():
        @pl.when(_expert_changed(texp_ref, t))
        def _():
            wg_bf[...] = wg_ref[...].astype(BF16)
            wu_bf[...] = wu_ref[...].astype(BF16)

        x = _unpack_halves(x_ref[...])
        gate = jnp.dot(x, wg_bf[...], preferred_element_type=F32) + bg_ref[...]
        up = jnp.dot(x, wu_bf[...], preferred_element_type=F32) + bu_ref[...]
        gate = jnp.minimum(gate, SW_LIMIT)
        up = jnp.clip(up, -SW_LIMIT, SW_LIMIT)
        act_ref[...] = ((up + 1.0) * gate * jax.nn.sigmoid(SW_ALPHA * gate)).astype(act_ref.dtype)

    @pl.when(t >= nused_ref[0])
    def _():
        act_ref[...] = jnp.zeros_like(act_ref)


def _gate_up(xs, texp, nused, w_gate_up, b_gate_up, layer):
    p_rows = xs.shape[0]
    ntiles = p_rows // MOE_TM
    tf = 1024
    nf = DFF // tf
    tile = lambda t, nu: jnp.minimum(t, nu[0] - 1)
    bgu = b_gate_up.reshape(N_LAYERS, N_EXP, 1, 2 * DFF)
    wspec = lambda off: pl.BlockSpec((None, None, D, tf),
                                     lambda f, t, te, nu: (layer, te[tile(t, nu)], 0, off + f))
    bspec = lambda off: pl.BlockSpec((None, None, 1, tf),
                                     lambda f, t, te, nu: (layer, te[tile(t, nu)], 0, off + f))
    return pl.pallas_call(
        _gate_up_body,
        grid_spec=pltpu.PrefetchScalarGridSpec(
            num_scalar_prefetch=2,
            grid=(nf, ntiles),
            in_specs=[pl.BlockSpec((MOE_TM, D // 2), lambda f, t, te, nu: (tile(t, nu), 0)),
                      wspec(0), wspec(nf), bspec(0), bspec(nf)],
            out_specs=pl.BlockSpec((MOE_TM, tf), lambda f, t, te, nu: (t, f)),
            scratch_shapes=[pltpu.VMEM((D, tf), BF16), pltpu.VMEM((D, tf), BF16)]),
        out_shape=jax.ShapeDtypeStruct((p_rows, DFF), BF16),
        compiler_params=_cp(2), name="moe_gate_up",
    )(texp, nused, xs, w_gate_up, w_gate_up, bgu, bgu)


def _down_body(texp_ref, nused_ref, a_ref, w_ref, b_ref, y_ref, w_bf):
    t = pl.program_id(1)

    @pl.when(t < nused_ref[0])
    def _():
        @pl.when(_expert_changed(texp_ref, t))
        def _():
            w_bf[...] = w_ref[...].astype(BF16)

        y_ref[...] = jnp.dot(a_ref[...], w_bf[...], preferred_element_type=F32) + b_ref[...]

    @pl.when(t >= nused_ref[0])
    def _():
        y_ref[...] = jnp.zeros_like(y_ref)


def _down(act, texp, nused, w_down, b_down, layer):
    p_rows = act.shape[0]
    ntiles = p_rows // MOE_TM
    tn = 1024
    tile = lambda t, nu: jnp.minimum(t, nu[0] - 1)
    bd = b_down.reshape(N_LAYERS, N_EXP, 1, D)
    return pl.pallas_call(
        _down_body,
        grid_spec=pltpu.PrefetchScalarGridSpec(
            num_scalar_prefetch=2,
            grid=(D // tn, ntiles),
            in_specs=[pl.BlockSpec((MOE_TM, DFF), lambda n, t, te, nu: (tile(t, nu), 0)),
                      pl.BlockSpec((None, None, DFF, tn), lambda n, t, te, nu: (layer, te[tile(t, nu)], 0, n)),
                      pl.BlockSpec((None, None, 1, tn), lambda n, t, te, nu: (layer, te[tile(t, nu)], 0, n))],
            out_specs=pl.BlockSpec((MOE_TM, tn), lambda n, t, te, nu: (t, n)),
            scratch_shapes=[pltpu.VMEM((DFF, tn), BF16)]),
        out_shape=jax.ShapeDtypeStruct((p_rows, D), F32),
        compiler_params=_cp(2), name="moe_down",
    )(texp, nused, act, w_down, bd)


def _combine_body(dest_hbm, ys_hbm, x_ref, wt_ref, gp_ref, gs_ref, o_ref, dsm, buf, sem_s, sem_r, *, ns):
    i = pl.program_id(0)
    cp = pltpu.make_async_copy(dest_hbm.at[pl.ds(i * ROUTE_CHUNK, ROUTE_CHUNK)], dsm, sem_s)
    cp.start()
    cp.wait()

    def row_copy(r, k, d):
        return pltpu.make_async_copy(ys_hbm.at[pl.ds(d, 1), :], buf.at[k, pl.ds(r, 1), :], sem_r)

    def gather(rows):
        def issue(r, c):
            for k in range(TOPK):
                row_copy(r, k, dsm[r * TOPK + k]).start()
            return c

        lax.fori_loop(0, rows, issue, 0)

        def drain(r, c):
            for k in range(TOPK):
                row_copy(0, 0, 0).wait()
            return c

        lax.fori_loop(0, rows, drain, 0)
        wt = wt_ref[:rows]
        moe = buf[0, :rows] * wt[:, 0:1]
        for k in range(1, TOPK):
            moe = moe + buf[k, :rows] * wt[:, k:k + 1]
        return moe

    def prompt():
        o_ref[...] = x_ref[...] + gp_ref[...] * gather(x_ref.shape[0])

    def sample():
        o_ref[:ns] = x_ref[:ns] + gs_ref[...] * gather(ns)
        o_ref[ns:] = jnp.zeros_like(o_ref[ns:])

    _prompt_or_sample(i, pl.num_programs(0), prompt, sample)


def _combine(x_all, ys, dest_pad, wts, mods, layer, np_rows, ns):
    nt = x_all.shape[0]
    assert np_rows % TOK_CHUNK == 0 and ns <= TOK_CHUNK
    any_spec = pl.BlockSpec(memory_space=pl.ANY)
    gp, gs = mods.specs(layer, 2, TOK_CHUNK)
    return pl.pallas_call(
        functools.partial(_combine_body, ns=ns),
        grid=(np_rows // TOK_CHUNK + 1,),
        in_specs=[any_spec, any_spec,
                  pl.BlockSpec((TOK_CHUNK, D), lambda i: (i, 0)),
                  pl.BlockSpec((TOK_CHUNK, TOPK), lambda i: (i, 0)),
                  gp, gs],
        out_specs=pl.BlockSpec((TOK_CHUNK, D), lambda i: (i, 0)),
        out_shape=jax.ShapeDtypeStruct((nt, D), F32),
        scratch_shapes=[pltpu.SMEM((ROUTE_CHUNK,), I32), pltpu.VMEM((TOPK, TOK_CHUNK, D), F32),
                        pltpu.SemaphoreType.DMA(()), pltpu.SemaphoreType.DMA(())],
        compiler_params=_cp(1), name="moe_combine",
    )(dest_pad, ys, x_all, wts, *mods.args)


def _moe(x_all, g, mods, layer, router_w, router_b, w_gate_up, b_gate_up, w_down, b_down, np_rows, ns):
    nt = x_all.shape[0]
    hp, idx, wts, rank, cnt = _route(x_all, g, mods, layer, router_w, router_b, np_rows, ns)
    counts = cnt[0, :N_EXP].astype(I32)
    padded = ((counts + MOE_TM - 1) // MOE_TM) * MOE_TM
    ends = jnp.cumsum(padded)
    gstart = ends - padded
    dest = gstart[idx] + rank
    n_entries = nt * TOPK
    n_chunks = -(-n_entries // ROUTE_CHUNK)
    dest_pad = jnp.zeros((n_chunks * ROUTE_CHUNK,), I32).at[:n_entries].set(dest.reshape(-1))
    ntiles = -(-n_entries // MOE_TM) + N_EXP
    tile_ends = ends // MOE_TM
    tile_ids = jnp.arange(ntiles, dtype=I32)
    texp = jnp.minimum(jnp.sum((tile_ids[:, None] >= tile_ends[None, :]).astype(I32), axis=1), N_EXP - 1)
    nused = tile_ends[-1:].astype(I32)
    last_tile = jnp.where(padded > 0, tile_ends - 1, -1).astype(I32)
    xs = _dispatch(hp, dest_pad, last_tile, nused, ntiles)
    act = _gate_up(xs, texp, nused, w_gate_up, b_gate_up, layer)
    ys = _down(act, texp, nused, w_down, b_down, layer)
    return _combine(x_all, ys, dest_pad, wts, mods, layer, np_rows, ns)


def _final_norm_body(x_ref, g_ref, o_ref):
    o_ref[...] = _normed(x_ref[...], g_ref[...])


def _final_norm(x_all, g, np_rows, ns):
    tr = 1024
    gspec = pl.BlockSpec((1, D), lambda i: (0, 0))
    yp = pl.pallas_call(
        _final_norm_body, grid=(np_rows // tr,),
        in_specs=[pl.BlockSpec((tr, D), lambda i: (i, 0)), gspec],
        out_specs=pl.BlockSpec((tr, D), lambda i: (i, 0)),
        out_shape=jax.ShapeDtypeStruct((np_rows, D), F32),
        compiler_params=_cp(1), name="final_norm_prompt",
    )(x_all, g)
    sblk = np_rows // ns
    ysm = pl.pallas_call(
        _final_norm_body, grid=(1,),
        in_specs=[pl.BlockSpec((ns, D), lambda i: (sblk, 0)), gspec],
        out_specs=pl.BlockSpec((ns, D), lambda i: (0, 0)),
        out_shape=jax.ShapeDtypeStruct((ns, D), F32),
        compiler_params=_cp(1), name="final_norm_sample",
    )(x_all, g)
    return yp, ysm


def kernel(x_prompt, x_sample, state_ret, cache_kv_w128_d1, cache_kv_w512_d4, cache_kv_w2048_d16, c_prompt, c_sample, norm1_g, ada1_w, ada1_b, w_in, w_out_ret, w_out_dil, w_out, norm2_g, ada2_w, ada2_b, router_w, router_b, w_gate_up, b_gate_up, w_down, b_down, final_norm_g):
    nb, seq, _ = x_prompt.shape
    ns, dec_seq, _ = x_sample.shape
    assert dec_seq == 1 and seq == 2048 and ns % 16 == 0
    past = cache_kv_w2048_d16.shape[2]
    np_rows = nb * seq
    assert np_rows % ns == 0
    caches = (cache_kv_w128_d1, cache_kv_w512_d4, cache_kv_w2048_d16)

    x_all = jnp.concatenate([x_prompt.reshape(np_rows, D), x_sample.reshape(ns, D)], axis=0)
    nc = -(-(ns + nb) // 16) * 16
    c_all = jnp.zeros((nc, D), F32).at[:ns].set(c_sample).at[ns:ns + nb].set(c_prompt)
    mods1 = _Mods(_mods(c_all, ada1_w, ada1_b), ns, nb, seq)
    mods2 = _Mods(_mods(c_all, ada2_w, ada2_b), ns, nb, seq)
    g1 = norm1_g.reshape(N_LAYERS, 1, D)
    g2 = norm2_g.reshape(N_LAYERS, 1, D)

    ret_p, ret_s, projs, kv_s = [], [], [], [[] for _ in DIL]
    for layer in range(N_LAYERS):
        h = _norm_mod(x_all, g1, mods1, layer, np_rows, ns)
        proj = _proj(h, w_in, layer)
        projs.append(proj)
        ya_p, st_p = _ret_prompt(proj, nb, seq)
        ya_s, st_s = _ret_sample(proj, state_ret, layer, np_rows, ns, past)
        yb_p = _dil_prompt(proj, nb, seq)
        yb_s = _dil_sample(proj, caches, layer, np_rows, ns)
        merged = _merge(ya_p, yb_p, ya_s, yb_s, proj, w_out_ret, w_out_dil, layer)
        x_all = _out_res(x_all, merged, w_out, mods1, layer, np_rows, ns)
        x_all = _moe(x_all, g2, mods2, layer, router_w, router_b, w_gate_up, b_gate_up, w_down, b_down,
                     np_rows, ns)
        ret_p.append(st_p)
        ret_s.append(st_s)
        for g in range(len(DIL)):
            c0 = OFF_DIL + g * 3 * DIL_W + DIL_W
            kv_s[g].append(proj[np_rows:, c0:c0 + 2 * DIL_W].reshape(ns, 1, 2, DIL_H, DIL_E))
    y_p, y_s = _final_norm(x_all, final_norm_g.reshape(1, D), np_rows, ns)
    kv_p = [_kv_out(projs, g, nb, seq) for g in range(len(DIL))]
    return (y_p.reshape(nb, seq, D), y_s.reshape(ns, 1, D),
            jnp.stack(ret_p), kv_p[0], kv_p[1], kv_p[2],
            jnp.stack(ret_s), jnp.stack(kv_s[0]), jnp.stack(kv_s[1]), jnp.stack(kv_s[2]))
```

```python
import functools

import jax
import jax.numpy as jnp
from jax import lax
from jax.experimental import pallas as pl
from jax.experimental.pallas import tpu as pltpu

F32 = jnp.float32
BF16 = jnp.bfloat16
U32 = jnp.uint32
I32 = jnp.int32

D = 2048
N_LAYERS = 2
RET_H = 8
RET_DK = 128
RET_DV = 256
RET_C = 128
ROPE_BASE = 10000.0
DIL = ((128, 1), (512, 4), (2048, 16))
DIL_H = 4
DIL_E = 128
DIL_W = DIL_H * DIL_E
N_EXP = 32
TOPK = 4
DFF = 2048
SW_LIMIT = 7.0
SW_ALPHA = 1.702
EPS = 1e-6

OFF_Q = 0
OFF_K = 1024
OFF_V = 2048
OFF_G = 4096
OFF_DIL = 6144
OFF_GA = 10752
OFF_GB = 12800
IN_W = 14848

VMEM_LIMIT = 56 * 1024 * 1024
MOE_TM = 512
ROUTE_CHUNK = 1024
TOK_CHUNK = ROUTE_CHUNK // TOPK


def _cp(n_grid_dims):
    return pltpu.CompilerParams(
        dimension_semantics=("arbitrary",) * n_grid_dims,
        vmem_limit_bytes=VMEM_LIMIT)


def _row_tile(n, cap):
    for k in range(1, n + 1):
        if n % k == 0 and n // k <= cap and (n // k) % 16 == 0:
            return n // k
    raise ValueError((n, cap))


def _silu(x):
    return x * jax.nn.sigmoid(x)


def _mods_body(c_ref, w_ref, b_ref, o_ref):
    a = _silu(c_ref[...]).astype(BF16)
    o_ref[...] = jnp.dot(a, w_ref[...].astype(BF16), preferred_element_type=F32) + b_ref[...]


def _mods(c_all, ada_w, ada_b):
    nc = c_all.shape[0]
    tn = 1024
    return pl.pallas_call(
        _mods_body,
        grid=(N_LAYERS, 3 * D // tn),
        in_specs=[pl.BlockSpec((nc, D), lambda l, n: (0, 0)),
                  pl.BlockSpec((None, D, tn), lambda l, n: (l, 0, n)),
                  pl.BlockSpec((None, 1, tn), lambda l, n: (l, 0, n))],
        out_specs=pl.BlockSpec((None, nc, tn), lambda l, n: (l, 0, n)),
        out_shape=jax.ShapeDtypeStruct((N_LAYERS, nc, 3 * D), F32),
        compiler_params=_cp(2), name="ada_mods",
    )(c_all, ada_w, ada_b.reshape(N_LAYERS, 1, 3 * D))


class _Mods:
    def __init__(self, m, ns, nb, seq):
        self.m3 = m
        self.m4 = m.reshape(m.shape[0], m.shape[1], 1, 3 * D)
        self.ns, self.nb, self.seq = ns, nb, seq

    @property
    def args(self):
        return (self.m4, self.m3)

    def specs(self, layer, which, tr, tn=D, col_axis=False):
        per = self.seq // tr
        ns, nb = self.ns, self.nb
        nblk = D // tn
        if col_axis:
            p_map = lambda n, i: (layer, ns + jnp.minimum(i // per, nb - 1), 0, which * nblk + n)
            s_map = lambda n, i: (layer, 0, which * nblk + n)
        else:
            p_map = lambda i: (layer, ns + jnp.minimum(i // per, nb - 1), 0, which * nblk)
            s_map = lambda i: (layer, 0, which * nblk)
        return [pl.BlockSpec((None, None, 1, tn), p_map), pl.BlockSpec((None, ns, tn), s_map)]


def _prompt_or_sample(step, n_steps, prompt_fn, sample_fn):
    pl.when(step < n_steps - 1)(prompt_fn)
    pl.when(step == n_steps - 1)(sample_fn)


def _normed(x, g):
    return x * lax.rsqrt(jnp.mean(x * x, axis=-1, keepdims=True) + EPS) * g


def _norm_mod_body(xp_ref, xs_ref, g_ref, shp_ref, scp_ref, shs_ref, scs_ref, o_ref, *, ns):
    def prompt():
        y = _normed(xp_ref[...], g_ref[...])
        o_ref[...] = (y * (1.0 + scp_ref[...]) + shp_ref[...]).astype(o_ref.dtype)

    def sample():
        y = _normed(xs_ref[...], g_ref[...])
        o_ref[:ns] = (y * (1.0 + scs_ref[...]) + shs_ref[...]).astype(o_ref.dtype)
        o_ref[ns:] = jnp.zeros_like(o_ref[ns:])

    _prompt_or_sample(pl.program_id(0), pl.num_programs(0), prompt, sample)


def _norm_mod(x_src, g, mods, layer, np_rows, ns):
    x_main, x_samp, sblk = x_src
    tr = 1024
    npt = np_rows // tr
    shp, shs = mods.specs(layer, 0, tr)
    scp, scs = mods.specs(layer, 1, tr)
    return pl.pallas_call(
        functools.partial(_norm_mod_body, ns=ns),
        grid=(npt + 1,),
        in_specs=[pl.BlockSpec((tr, D), lambda i: (jnp.minimum(i, npt - 1), 0)),
                  pl.BlockSpec((ns, D), lambda i: (sblk, 0)),
                  pl.BlockSpec((None, 1, D), lambda i: (layer, 0, 0)),
                  shp, scp, shs, scs],
        out_specs=pl.BlockSpec((tr, D), lambda i: (i, 0)),
        out_shape=jax.ShapeDtypeStruct((np_rows + ns, D), BF16),
        compiler_params=_cp(1), name="norm1",
    )(x_main, x_samp, g, mods.m4, mods.m4, mods.m3, mods.m3)


def _proj_body(x_ref, w_ref, o_ref, wbf_ref):
    @pl.when(pl.program_id(1) == 0)
    def _():
        wbf_ref[...] = w_ref[...].astype(BF16)

    o_ref[...] = jnp.dot(x_ref[...], wbf_ref[...], preferred_element_type=F32).astype(o_ref.dtype)


def _proj(h, w_in, layer):
    nt = h.shape[0]
    tm = _row_tile(nt, 1376)
    tn = 1024
    return pl.pallas_call(
        _proj_body,
        grid=(pl.cdiv(IN_W, tn), nt // tm),
        in_specs=[pl.BlockSpec((tm, D), lambda n, m: (m, 0)),
                  pl.BlockSpec((None, D, tn), lambda n, m: (layer, 0, n))],
        out_specs=pl.BlockSpec((tm, tn), lambda n, m: (m, n)),
        out_shape=jax.ShapeDtypeStruct((nt, IN_W), F32),
        scratch_shapes=[pltpu.VMEM((D, tn), BF16)],
        compiler_params=_cp(2), name="in_proj",
    )(h, w_in)


def _rope_tables(pos):
    half = RET_DK // 2
    inv = ROPE_BASE ** -jnp.linspace(0.0, 1.0, half, dtype=F32)
    ang = pos.astype(F32)[:, None] * inv[None, :]
    cos = jnp.cos(ang)
    sin = jnp.sin(ang)
    return jnp.concatenate([cos, cos], axis=-1), jnp.concatenate([-sin, sin], axis=-1)


def _decay_tables(chunk):
    log_g = jnp.log1p(-jnp.exp2(-5.0 - jnp.arange(RET_H, dtype=F32)))
    idx = jnp.arange(chunk, dtype=F32)
    rel = idx[:, None] - idx[None, :]
    inner = jnp.where(rel >= 0, jnp.exp(log_g[:, None, None] * jnp.maximum(rel, 0.0)), 0.0)
    q_decay = jnp.exp(log_g[:, None] * (idx + 1.0)[None, :])
    k_decay = jnp.exp(log_g[:, None] * (chunk - 1.0 - idx)[None, :])
    chunk_decay = jnp.exp(log_g * chunk)
    return inner, q_decay, k_decay, chunk_decay


def _rot(x, cos, sin):
    return x * cos + pltpu.roll(x, RET_DK // 2, 1) * sin


def _head_norm_gate(o, g):
    on = o * lax.rsqrt(jnp.mean(o * o, axis=-1, keepdims=True) + EPS)
    return _silu(g) * on


def _ret_prompt_body(q_ref, k_ref, v_ref, g_ref, cos_ref, sin_ref, inner_ref, qd_ref, kd_ref, cd_ref,
                     ya_ref, s_ref):
    @pl.when(pl.program_id(1) == 0)
    def _():
        s_ref[...] = jnp.zeros_like(s_ref)

    cos = cos_ref[...]
    sin = sin_ref[...]
    for h in range(RET_H):
        q = _rot(q_ref[:, h * RET_DK:(h + 1) * RET_DK], cos, sin)
        k = _rot(k_ref[:, h * RET_DK:(h + 1) * RET_DK], cos, sin) * (RET_DK ** -0.5)
        vb = v_ref[:, h * RET_DV:(h + 1) * RET_DV].astype(BF16)
        s = lax.dot_general(q.astype(BF16), k.astype(BF16), (((1,), (1,)), ((), ())),
                            preferred_element_type=F32) * inner_ref[h]
        state = s_ref[h]
        o = (jnp.dot(s.astype(BF16), vb, preferred_element_type=F32)
             + jnp.dot((q * qd_ref[h]).astype(BF16), state.astype(BF16), preferred_element_type=F32))
        kt = (k * kd_ref[h]).T.astype(BF16)
        s_ref[h] = state * cd_ref[h] + jnp.dot(kt, vb, preferred_element_type=F32)
        y = _head_norm_gate(o, g_ref[:, h * RET_DV:(h + 1) * RET_DV])
        ya_ref[:, h * RET_DV:(h + 1) * RET_DV] = y.astype(ya_ref.dtype)


def _ret_prompt(proj, nb, seq):
    nch = seq // RET_C
    cos, sin = _rope_tables(jnp.arange(seq, dtype=I32))
    inner, qd, kd, cd = _decay_tables(RET_C)
    qd_b = jnp.broadcast_to(qd[:, :, None], (RET_H, RET_C, RET_DK))
    kd_b = jnp.broadcast_to(kd[:, :, None], (RET_H, RET_C, RET_DK))
    cd_b = jnp.broadcast_to(cd[:, None, None], (RET_H, 1, RET_DV))
    row = lambda b, c: b * nch + c
    full3 = lambda shp: pl.BlockSpec(shp, lambda b, c: (0, 0, 0))
    return pl.pallas_call(
        _ret_prompt_body,
        grid=(nb, nch),
        in_specs=[pl.BlockSpec((RET_C, 1024), lambda b, c: (row(b, c), OFF_Q // 1024)),
                  pl.BlockSpec((RET_C, 1024), lambda b, c: (row(b, c), OFF_K // 1024)),
                  pl.BlockSpec((RET_C, 2048), lambda b, c: (row(b, c), OFF_V // 2048)),
                  pl.BlockSpec((RET_C, 2048), lambda b, c: (row(b, c), OFF_G // 2048)),
                  pl.BlockSpec((RET_C, RET_DK), lambda b, c: (c, 0)),
                  pl.BlockSpec((RET_C, RET_DK), lambda b, c: (c, 0)),
                  full3((RET_H, RET_C, RET_C)), full3((RET_H, RET_C, RET_DK)),
                  full3((RET_H, RET_C, RET_DK)), full3((RET_H, 1, RET_DV))],
        out_specs=[pl.BlockSpec((RET_C, D), lambda b, c: (row(b, c), 0)),
                   pl.BlockSpec((None, RET_H, RET_DK, RET_DV), lambda b, c: (b, 0, 0, 0))],
        out_shape=[jax.ShapeDtypeStruct((nb * seq, D), BF16),
                   jax.ShapeDtypeStruct((nb, RET_H, RET_DK, RET_DV), F32)],
        compiler_params=_cp(2), name="retention_prompt",
    )(proj, proj, proj, proj, cos, sin, inner, qd_b, kd_b, cd_b)


RS_NB = 8


def _ret_sample_body(q_ref, k_ref, v_ref, g_ref, cos_ref, sin_ref, inner_ref, qd_ref, kd_ref, cd_ref,
                     st_ref, ya_ref, so_ref):
    i = pl.program_id(0)
    cos = cos_ref[...]
    sin = sin_ref[...]
    row_i = lax.broadcasted_iota(I32, (RET_DK, RET_DK), 0)
    col_i = lax.broadcasted_iota(I32, (RET_DK, RET_DK), 1)
    eye = row_i == col_i
    rows = pl.ds(pl.multiple_of(i * RS_NB, RS_NB), RS_NB)
    q_all = q_ref[rows, :]
    k_all = k_ref[rows, :]
    v_all = v_ref[rows, :]
    g_all = g_ref[rows, :]
    out_rows = []
    for j in range(RS_NB):
        heads = []
        for h in range(RET_H):
            q = _rot(q_all[j:j + 1, h * RET_DK:(h + 1) * RET_DK], cos, sin)
            k = _rot(k_all[j:j + 1, h * RET_DK:(h + 1) * RET_DK], cos, sin) * (RET_DK ** -0.5)
            v = v_all[j:j + 1, h * RET_DV:(h + 1) * RET_DV]
            s = jnp.sum(q * k, axis=-1, keepdims=True) * inner_ref[h]
            state = st_ref[j, h]
            q8 = jnp.broadcast_to(q * qd_ref[h], (8, RET_DK)).astype(BF16)
            qs = jnp.dot(q8, state.astype(BF16), preferred_element_type=F32)[0:1]
            o = s * v + qs
            kdiag = jnp.where(eye, jnp.broadcast_to(k * kd_ref[h], (RET_DK, RET_DK)), 0.0).astype(BF16)
            vb = jnp.broadcast_to(v, (RET_DK, RET_DV)).astype(BF16)
            so_ref[j, h] = state * cd_ref[h] + jnp.dot(kdiag, vb, preferred_element_type=F32)
            heads.append(_head_norm_gate(o, g_all[j:j + 1, h * RET_DV:(h + 1) * RET_DV]))
        out_rows.append(jnp.concatenate(heads, axis=1))
    ya_ref[...] = jnp.concatenate(out_rows, axis=0)


def _ret_sample(proj, state_ret, layer, np_rows, ns, past):
    cos, sin = _rope_tables(past + jnp.arange(1, dtype=I32))
    inner, qd, kd, cd = _decay_tables(1)
    inner_b = inner.reshape(RET_H, 1, 1)
    qd_b = jnp.broadcast_to(qd[:, :, None], (RET_H, 1, RET_DK))
    kd_b = jnp.broadcast_to(kd[:, :, None], (RET_H, 1, RET_DK))
    cd_b = jnp.broadcast_to(cd[:, None, None], (RET_H, 1, RET_DV))
    sblk = np_rows // ns
    full = lambda shp: pl.BlockSpec(shp, lambda i: (0,) * len(shp))
    return pl.pallas_call(
        _ret_sample_body,
        grid=(ns // RS_NB,),
        in_specs=[pl.BlockSpec((ns, 1024), lambda i: (sblk, OFF_Q // 1024)),
                  pl.BlockSpec((ns, 1024), lambda i: (sblk, OFF_K // 1024)),
                  pl.BlockSpec((ns, 2048), lambda i: (sblk, OFF_V // 2048)),
                  pl.BlockSpec((ns, 2048), lambda i: (sblk, OFF_G // 2048)),
                  full((1, RET_DK)), full((1, RET_DK)),
                  full((RET_H, 1, 1)), full((RET_H, 1, RET_DK)), full((RET_H, 1, RET_DK)),
                  full((RET_H, 1, RET_DV)),
                  pl.BlockSpec((None, RS_NB, RET_H, RET_DK, RET_DV), lambda i: (layer, i, 0, 0, 0))],
        out_specs=[pl.BlockSpec((RS_NB, D), lambda i: (i, 0)),
                   pl.BlockSpec((RS_NB, RET_H, RET_DK, RET_DV), lambda i: (i, 0, 0, 0))],
        out_shape=[jax.ShapeDtypeStruct((ns, D), F32),
                   jax.ShapeDtypeStruct((ns, RET_H, RET_DK, RET_DV), F32)],
        compiler_params=_cp(1), name="retention_sample",
    )(proj, proj, proj, proj, cos, sin, inner_b, qd_b, kd_b, cd_b, state_ret)


def _merge_groups(parts):
    m_all = parts[0][1]
    for _, m_g, _ in parts[1:]:
        m_all = jnp.maximum(m_all, m_g)
    num = 0.0
    den = 0.0
    for n_g, m_g, l_g in parts:
        w = jnp.exp(m_g - m_all)
        num = num + n_g * w
        den = den + l_g * w
    return num / den


def _dil_prompt_body(*refs, seq):
    qkv = refs[:9]
    yb_ref = refs[9]
    num_ref, m_ref, l_ref = refs[10:13]
    band = 128
    scale = DIL_E ** -0.5
    qi = lax.broadcasted_iota(I32, (band, band), 0)
    kj1 = lax.broadcasted_iota(I32, (band, band), 1)
    qi2 = lax.broadcasted_iota(I32, (band, 2 * band), 0)
    kj2 = lax.broadcasted_iota(I32, (band, 2 * band), 1)
    first_valid = kj1 <= qi
    later_valid = (kj2 >= qi2) & (kj2 <= qi2 + band)
    for g, (window, dil) in enumerate(DIL):
        assert window // dil == band
        q_ref, k_ref, v_ref = qkv[3 * g:3 * g + 3]
        nblk = seq // dil // band
        for p in range(dil):
            for n in range(nblk):
                rows_q = pl.ds(p + n * band * dil, band, stride=dil)
                qb = q_ref[rows_q, :].astype(BF16)
                if n == 0:
                    rows_k = rows_q
                    valid = first_valid
                else:
                    rows_k = pl.ds(p + (n - 1) * band * dil, 2 * band, stride=dil)
                    valid = later_valid
                kk = k_ref[rows_k, :].astype(BF16)
                vv = v_ref[rows_k, :].astype(BF16)
                s = lax.dot_general(qb, kk, (((1,), (1,)), ((), ())), preferred_element_type=F32) * scale
                s = jnp.where(valid, s, -jnp.inf)
                m = jnp.max(s, axis=-1, keepdims=True)
                pe = jnp.exp(s - m)
                l = jnp.sum(pe, axis=-1, keepdims=True)
                num_ref[g, rows_q, :] = jnp.dot(pe.astype(BF16), vv, preferred_element_type=F32)
                m_ref[g, rows_q, :] = jnp.broadcast_to(m, (band, DIL_E))
                l_ref[g, rows_q, :] = jnp.broadcast_to(l, (band, DIL_E))
    parts = [(num_ref[g], m_ref[g], l_ref[g]) for g in range(len(DIL))]
    yb_ref[...] = _merge_groups(parts).astype(yb_ref.dtype)


def _dil_prompt(proj, nb, seq):
    in_specs = []
    for g in range(len(DIL)):
        for j in range(3):
            cb = (OFF_DIL + g * 3 * DIL_W + j * DIL_W) // DIL_E
            in_specs.append(pl.BlockSpec((seq, DIL_E), lambda b, h, cb=cb: (b, cb + h)))
    scr = pltpu.VMEM((len(DIL), seq, DIL_E), F32)
    return pl.pallas_call(
        functools.partial(_dil_prompt_body, seq=seq),
        grid=(nb, DIL_H),
        in_specs=in_specs,
        out_specs=pl.BlockSpec((seq, DIL_E), lambda b, h: (b, h)),
        out_shape=jax.ShapeDtypeStruct((nb * seq, DIL_W), BF16),
        scratch_shapes=[scr, scr, scr],
        compiler_params=_cp(2), name="dilated_prompt",
    )(*([proj] * 9))


DS_NB = 8


def _dil_sample_body(c0_ref, c1_ref, c2_ref, q0_ref, q1_ref, q2_ref, yb_ref):
    i = pl.program_id(0)
    caches = (c0_ref, c1_ref, c2_ref)
    scale = DIL_E ** -0.5
    rows = pl.ds(pl.multiple_of(i * DS_NB, DS_NB), DS_NB)
    news = [ref[rows, :] for ref in (q0_ref, q1_ref, q2_ref)]
    out_rows = []
    for j in range(DS_NB):
        parts = []
        for g in range(len(DIL)):
            new = news[g][j:j + 1]
            q4 = jnp.concatenate([new[:, h * DIL_E:(h + 1) * DIL_E] for h in range(DIL_H)], axis=0)
            k4 = jnp.concatenate([new[:, DIL_W + h * DIL_E:DIL_W + (h + 1) * DIL_E] for h in range(DIL_H)], axis=0)
            v4 = jnp.concatenate([new[:, 2 * DIL_W + h * DIL_E:2 * DIL_W + (h + 1) * DIL_E] for h in range(DIL_H)],
                                 axis=0)
            kc = caches[g][j, :, 0:DIL_H, :]
            vc = caches[g][j, :, DIL_H:2 * DIL_H, :]
            s_c = jnp.sum(kc * q4[None], axis=-1, keepdims=True) * scale
            s_n = jnp.sum(k4 * q4, axis=-1, keepdims=True) * scale
            m = jnp.maximum(jnp.max(s_c, axis=0), s_n)
            p_c = jnp.exp(s_c - m[None])
            p_n = jnp.exp(s_n - m)
            l = jnp.sum(p_c, axis=0) + p_n
            num = jnp.sum(p_c * vc, axis=0) + p_n * v4
            parts.append((num, m, l))
        y = _merge_groups(parts)
        out_rows.append(jnp.concatenate([y[h:h + 1] for h in range(DIL_H)], axis=1))
    yb_ref[...] = jnp.concatenate(out_rows, axis=0)


def _dil_sample(proj, caches, layer, np_rows, ns):
    sblk = np_rows // ns
    views = []
    in_specs = []
    for cache, (window, dil) in zip(caches, DIL):
        wb = cache.shape[2]
        assert wb == window and wb % dil == 0
        views.append(cache.reshape(cache.shape[0] * ns, wb // dil, dil, 2 * DIL_H, DIL_E))
        in_specs.append(pl.BlockSpec((DS_NB, wb // dil, None, 2 * DIL_H, DIL_E),
                                     lambda i: (layer * (ns // DS_NB) + i, 0, 0, 0, 0)))
    for g in range(len(DIL)):
        in_specs.append(pl.BlockSpec((ns, 3 * DIL_W), lambda i, g=g: (sblk, OFF_DIL // (3 * DIL_W) + g)))
    return pl.pallas_call(
        _dil_sample_body,
        grid=(ns // DS_NB,),
        in_specs=in_specs,
        out_specs=pl.BlockSpec((DS_NB, DIL_W), lambda i: (i, 0)),
        out_shape=jax.ShapeDtypeStruct((ns, DIL_W), F32),
        compiler_params=_cp(1), name="dilated_sample",
    )(*views, proj, proj, proj)


def _kv_out_body(*refs):
    o_ref = refs[-1]
    layer = pl.program_id(0)
    for l in range(N_LAYERS):
        @pl.when(layer == l)
        def _(k_ref=refs[2 * l], v_ref=refs[2 * l + 1]):
            for h in range(DIL_H):
                o_ref[:, 0, h, :] = k_ref[:, h * DIL_E:(h + 1) * DIL_E]
                o_ref[:, 1, h, :] = v_ref[:, h * DIL_E:(h + 1) * DIL_E]


def _kv_out(projs, g, nb, seq):
    window = min(DIL[g][0], seq)
    rows = min(window, 512)
    nr = window // rows
    cb = (OFF_DIL + g * 3 * DIL_W + DIL_W) // DIL_W
    first = (seq - window) // rows
    per_b = seq // rows
    last = (nb - 1) * per_b + first + nr - 1
    in_specs = []
    for l in range(N_LAYERS):
        def rblk(ll, b, r, l=l):
            cur = b * per_b + first + r
            return jnp.where(ll == l, cur, jnp.where(ll < l, first, last))
        in_specs.append(pl.BlockSpec((rows, DIL_W), lambda ll, b, r, f=rblk: (f(ll, b, r), cb)))
        in_specs.append(pl.BlockSpec((rows, DIL_W), lambda ll, b, r, f=rblk: (f(ll, b, r), cb + 1)))
    args = []
    for p in projs:
        args += [p, p]
    return pl.pallas_call(
        _kv_out_body,
        grid=(N_LAYERS, nb, nr),
        in_specs=in_specs,
        out_specs=pl.BlockSpec((None, None, rows, 2, DIL_H, DIL_E), lambda ll, b, r: (ll, b, r, 0, 0, 0)),
        out_shape=jax.ShapeDtypeStruct((N_LAYERS, nb, window, 2, DIL_H, DIL_E), F32),
        compiler_params=_cp(3), name="kv_prompt_out",
    )(*args)


W_CHUNK = 256


def _load_weight_bf16(w_hbm, layer, dst_bf, stage, sems):
    n = dst_bf.shape[0] // W_CHUNK

    def chunk(c):
        return pltpu.make_async_copy(w_hbm.at[layer, pl.ds(c * W_CHUNK, W_CHUNK), :], stage.at[c % 2],
                                     sems.at[c % 2])

    chunk(0).start()
    for c in range(n):
        if c + 1 < n:
            chunk(c + 1).start()
        chunk(c).wait()
        dst_bf[c * W_CHUNK:(c + 1) * W_CHUNK, :] = stage[c % 2].astype(BF16)


GATE_TN = 512


def _mix_out_body(xp_ref, xs_ref, yap_ref, ybp_ref, yas_ref, ybs_ref, *rest, ns, layer):
    n_gate = D // GATE_TN
    ga_refs, gb_refs = rest[:n_gate], rest[n_gate:2 * n_gate]
    gp_ref, gs_ref, wr_hbm, wd_hbm, wo_hbm, o_ref, wr_bf, wd_bf, wo_bf, stage, sems = rest[2 * n_gate:]

    @pl.when(pl.program_id(0) == 0)
    def _():
        _load_weight_bf16(wr_hbm, layer, wr_bf, stage, sems)
        _load_weight_bf16(wd_hbm, layer, wd_bf, stage, sems)
        _load_weight_bf16(wo_hbm, layer, wo_bf, stage, sems)

    def mixed(rows, ya, yb, x, gate):
        parts = []
        for c in range(n_gate):
            cols = slice(c * GATE_TN, (c + 1) * GATE_TN)
            y_a = jnp.dot(ya, wr_bf[:, cols], preferred_element_type=F32)
            y_b = jnp.dot(yb, wd_bf[:, cols], preferred_element_type=F32)
            merged = jax.nn.sigmoid(ga_refs[c][:rows]) * y_a + jax.nn.sigmoid(gb_refs[c][:rows]) * y_b
            parts.append(merged.astype(BF16))
        mix = jnp.dot(jnp.concatenate(parts, axis=1), wo_bf[...], preferred_element_type=F32)
        return x + gate * mix

    def prompt():
        o_ref[...] = mixed(o_ref.shape[0], yap_ref[...], ybp_ref[...], xp_ref[...], gp_ref[...])

    def sample():
        o_ref[:ns] = mixed(ns, yas_ref[...].astype(BF16), ybs_ref[...].astype(BF16), xs_ref[...], gs_ref[...])
        o_ref[ns:] = jnp.zeros_like(o_ref[ns:])

    _prompt_or_sample(pl.program_id(0), pl.num_programs(0), prompt, sample)


def _mix_out(x_src, ya_p, yb_p, ya_s, yb_s, proj, w_out_ret, w_out_dil, w_out, mods, layer):
    x_main, x_samp, sblk = x_src
    nt = proj.shape[0]
    np_rows = ya_p.shape[0]
    ns = ya_s.shape[0]
    tm = 256
    npt = np_rows // tm
    ptile = lambda i: jnp.minimum(i, npt - 1)
    gp, gs = mods.specs(layer, 2, tm)
    any_spec = pl.BlockSpec(memory_space=pl.ANY)
    gate_specs = [pl.BlockSpec((tm, GATE_TN), lambda i, cb=(off // GATE_TN + c): (i, cb))
                  for off in (OFF_GA, OFF_GB) for c in range(D // GATE_TN)]
    return pl.pallas_call(
        functools.partial(_mix_out_body, ns=ns, layer=layer),
        grid=(npt + 1,),
        in_specs=[pl.BlockSpec((tm, D), lambda i: (ptile(i), 0)),
                  pl.BlockSpec((ns, D), lambda i: (sblk, 0)),
                  pl.BlockSpec((tm, D), lambda i: (ptile(i), 0)),
                  pl.BlockSpec((tm, DIL_W), lambda i: (ptile(i), 0)),
                  pl.BlockSpec((ns, D), lambda i: (0, 0)),
                  pl.BlockSpec((ns, DIL_W), lambda i: (0, 0)),
                  *gate_specs, gp, gs, any_spec, any_spec, any_spec],
        out_specs=pl.BlockSpec((tm, D), lambda i: (i, 0)),
        out_shape=jax.ShapeDtypeStruct((nt, D), F32),
        scratch_shapes=[pltpu.VMEM((D, D), BF16), pltpu.VMEM((DIL_W, D), BF16), pltpu.VMEM((D, D), BF16),
                        pltpu.VMEM((2, W_CHUNK, D), F32), pltpu.SemaphoreType.DMA((2,))],
        compiler_params=_cp(1), name="mixer_out",
    )(x_main, x_samp, ya_p, yb_p, ya_s, yb_s, *([proj] * (2 * (D // GATE_TN))), *mods.args,
      w_out_ret, w_out_dil, w_out)


def _pack_halves(h):
    a = lax.bitcast_convert_type(h[:, :D // 2].astype(BF16).astype(F32), U32)
    b = lax.bitcast_convert_type(h[:, D // 2:].astype(BF16).astype(F32), U32)
    return a | (b >> 16)


def _unpack_halves(w):
    a = lax.bitcast_convert_type(w & jnp.uint32(0xFFFF0000), F32)
    b = lax.bitcast_convert_type(w << 16, F32)
    return jnp.concatenate([a.astype(BF16), b.astype(BF16)], axis=1)


def _route_rows(x, g, sh, sc, rw, rb, cnt):
    tr = x.shape[0]
    h = _normed(x, g) * (1.0 + sc) + sh
    logits = jnp.dot(h, rw, precision=lax.Precision.HIGHEST, preferred_element_type=F32) + rb
    lane = lax.broadcasted_iota(I32, logits.shape, 1)
    work = logits
    tops = []
    member = jnp.zeros(logits.shape, F32)
    for _ in range(TOPK):
        m = jnp.max(work, axis=-1, keepdims=True)
        sel = jnp.min(jnp.where(work == m, lane, logits.shape[1]), axis=-1, keepdims=True)
        hit = lane == sel
        work = jnp.where(hit, -jnp.inf, work)
        member = jnp.where(hit, 1.0, member)
        tops.append((m, sel, hit))
    ri = lax.broadcasted_iota(I32, (tr, tr), 0)
    ci = lax.broadcasted_iota(I32, (tr, tr), 1)
    tri = jnp.where(ci < ri, 1.0, 0.0).astype(BF16)
    before = jnp.dot(tri, member.astype(BF16), preferred_element_type=F32) + cnt
    es = [jnp.exp(m - tops[0][0]) for m, _, _ in tops]
    den = es[0]
    for e in es[1:]:
        den = den + e
    choices = []
    for r, (m, sel, hit) in enumerate(tops):
        rank = jnp.sum(jnp.where(hit, before, 0.0), axis=-1, keepdims=True).astype(I32)
        choices.append((sel, es[r] / den, rank))
    return _pack_halves(h), choices, jnp.sum(member, axis=0, keepdims=True)


def _route_body(x_ref, g_ref, shp_ref, scp_ref, shs_ref, scs_ref, rw_ref, rb_ref,
                hp_ref, idx_ref, wt_ref, rank_ref, cnt_ref, *, ns):
    @pl.when(pl.program_id(0) == 0)
    def _():
        cnt_ref[...] = jnp.zeros_like(cnt_ref)

    def emit(rows, x, sh, sc):
        hp, choices, tile_cnt = _route_rows(x, g_ref[...], sh, sc, rw_ref[...], rb_ref[...], cnt_ref[...])
        cnt_ref[...] = cnt_ref[...] + tile_cnt
        hp_ref[:rows] = hp
        for r, (sel, wt, rank) in enumerate(choices):
            idx_ref[:rows, r:r + 1] = sel
            wt_ref[:rows, r:r + 1] = wt
            rank_ref[:rows, r:r + 1] = rank

    def prompt():
        emit(x_ref.shape[0], x_ref[...], shp_ref[...], scp_ref[...])

    def sample():
        emit(ns, x_ref[:ns], shs_ref[...], scs_ref[...])
        hp_ref[ns:] = jnp.zeros_like(hp_ref[ns:])
        idx_ref[ns:] = jnp.zeros_like(idx_ref[ns:])
        wt_ref[ns:] = jnp.zeros_like(wt_ref[ns:])
        rank_ref[ns:] = jnp.zeros_like(rank_ref[ns:])

    _prompt_or_sample(pl.program_id(0), pl.num_programs(0), prompt, sample)


def _route(x_all, g, mods, layer, router_w, router_b, np_rows, ns):
    nt = x_all.shape[0]
    tr = 512
    lanes = 128
    rw = jnp.zeros((D, lanes), F32).at[:, :N_EXP].set(router_w[layer])
    rb = jnp.full((1, lanes), -jnp.inf, F32).at[0, :N_EXP].set(router_b[layer])
    shp, shs = mods.specs(layer, 0, tr)
    scp, scs = mods.specs(layer, 1, tr)
    full = lambda shp_: pl.BlockSpec(shp_, lambda i: (0,) * len(shp_))
    rows = lambda w: pl.BlockSpec((tr, w), lambda i: (i, 0))
    return pl.pallas_call(
        functools.partial(_route_body, ns=ns),
        grid=(np_rows // tr + 1,),
        in_specs=[rows(D), pl.BlockSpec((None, 1, D), lambda i: (layer, 0, 0)),
                  shp, scp, shs, scs, full((D, lanes)), full((1, lanes))],
        out_specs=[rows(D // 2), rows(TOPK), rows(TOPK), rows(TOPK), full((1, lanes))],
        out_shape=[jax.ShapeDtypeStruct((nt, D // 2), U32), jax.ShapeDtypeStruct((nt, TOPK), I32),
                   jax.ShapeDtypeStruct((nt, TOPK), F32), jax.ShapeDtypeStruct((nt, TOPK), I32),
                   jax.ShapeDtypeStruct((1, lanes), F32)],
        compiler_params=_cp(1), name="norm2_route",
    )(x_all, g, mods.m4, mods.m4, mods.m3, mods.m3, rw, rb)


def _route_chunk(dest_hbm, dsm, sems, step, n_steps):
    def fetch(s, slot):
        return pltpu.make_async_copy(dest_hbm.at[pl.ds(s * ROUTE_CHUNK, ROUTE_CHUNK)],
                                     dsm.at[pl.ds(slot * ROUTE_CHUNK, ROUTE_CHUNK)], sems.at[slot])

    slot = step % 2

    @pl.when(step == 0)
    def _():
        fetch(0, 0).start()

    fetch(step, slot).wait()

    @pl.when(step + 1 < n_steps)
    def _():
        fetch(step + 1, 1 - slot).start()

    return slot * ROUTE_CHUNK


def _dispatch_body(last_ref, nused_ref, dest_hbm, hp_ref, xs_hbm, dsm, zeros_ref, sem_s, sem_z, sem_r,
                   *, nt, ntiles):
    i = pl.program_id(0)

    def zero_tile(t):
        return pltpu.make_async_copy(zeros_ref, xs_hbm.at[pl.ds(t * MOE_TM, MOE_TM), :], sem_z)

    @pl.when(i == 0)
    def _():
        zeros_ref[...] = jnp.zeros_like(zeros_ref)
        for e in range(N_EXP):
            @pl.when(last_ref[e] >= 0)
            def _():
                zero_tile(last_ref[e]).start()

        def fill(t, c):
            zero_tile(t).start()
            return c

        lax.fori_loop(nused_ref[0], ntiles, fill, 0)
        for e in range(N_EXP):
            @pl.when(last_ref[e] >= 0)
            def _():
                zero_tile(0).wait()

        def fill_wait(t, c):
            zero_tile(0).wait()
            return c

        lax.fori_loop(nused_ref[0], ntiles, fill_wait, 0)

    slot = _route_chunk(dest_hbm, dsm, sem_s, i, pl.num_programs(0))
    rows = jnp.minimum(TOK_CHUNK, nt - i * TOK_CHUNK)

    def row_copy(r, d):
        return pltpu.make_async_copy(hp_ref.at[pl.ds(r, 1), :], xs_hbm.at[pl.ds(d, 1), :], sem_r)

    def issue(r, c):
        for k in range(TOPK):
            row_copy(r, dsm[slot + r * TOPK + k]).start()
        return c

    lax.fori_loop(0, rows, issue, 0)

    def drain(r, c):
        for k in range(TOPK):
            row_copy(0, 0).wait()
        return c

    lax.fori_loop(0, rows, drain, 0)


def _dispatch(hp, dest_pad, last_tile, nused, ntiles):
    nt = hp.shape[0]
    steps = dest_pad.shape[0] // ROUTE_CHUNK
    return pl.pallas_call(
        functools.partial(_dispatch_body, nt=nt, ntiles=ntiles),
        grid_spec=pltpu.PrefetchScalarGridSpec(
            num_scalar_prefetch=2,
            grid=(steps,),
            in_specs=[pl.BlockSpec(memory_space=pl.ANY),
                      pl.BlockSpec((TOK_CHUNK, D // 2), lambda i, last, nu: (i, 0))],
            out_specs=pl.BlockSpec(memory_space=pl.ANY),
            scratch_shapes=[pltpu.SMEM((2 * ROUTE_CHUNK,), I32),
                            pltpu.VMEM((MOE_TM, D // 2), U32),
                            pltpu.SemaphoreType.DMA((2,)), pltpu.SemaphoreType.DMA(()),
                            pltpu.SemaphoreType.DMA(())]),
        out_shape=jax.ShapeDtypeStruct((ntiles * MOE_TM, D // 2), U32),
        compiler_params=_cp(1), name="moe_dispatch",
    )(last_tile, nused, dest_pad, hp)


def _expert_changed(texp_ref, t):
    prev = texp_ref[jnp.maximum(t - 1, 0)]
    return (t == 0) | (texp_ref[t] != prev)


def _gate_up_body(texp_ref, nused_ref, x_ref, wg_ref, wu_ref, bg_ref, bu_ref, act_ref, wg_bf, wu_bf):
    t = pl.program_id(1)

    @pl.when(t < nused_ref[0])
    def _():
        @pl.when(_expert_changed(texp_ref, t))
        def _():
            wg_bf[...] = wg_ref[...].astype(BF16)
            wu_bf[...] = wu_ref[...].astype(BF16)

        x = _unpack_halves(x_ref[...])
        gate = jnp.dot(x, wg_bf[...], preferred_element_type=F32) + bg_ref[...]
        up = jnp.dot(x, wu_bf[...], preferred_element_type=F32) + bu_ref[...]
        gate = jnp.minimum(gate, SW_LIMIT)
        up = jnp.clip(up, -SW_LIMIT, SW_LIMIT)
        act_ref[...] = ((up + 1.0) * gate * jax.nn.sigmoid(SW_ALPHA * gate)).astype(act_ref.dtype)

    @pl.when(t >= nused_ref[0])
    def _():
        act_ref[...] = jnp.zeros_like(act_ref)


def _gate_up(xs, texp, nused, w_gate_up, b_gate_up, layer):
    p_rows = xs.shape[0]
    ntiles = p_rows // MOE_TM
    tf = 1024
    nf = DFF // tf
    tile = lambda t, nu: jnp.minimum(t, nu[0] - 1)
    bgu = b_gate_up.reshape(N_LAYERS, N_EXP, 1, 2 * DFF)
    wspec = lambda off: pl.BlockSpec((None, None, D, tf),
                                     lambda f, t, te, nu: (layer, te[tile(t, nu)], 0, off + f))
    bspec = lambda off: pl.BlockSpec((None, None, 1, tf),
                                     lambda f, t, te, nu: (layer, te[tile(t, nu)], 0, off + f))
    return pl.pallas_call(
        _gate_up_body,
        grid_spec=pltpu.PrefetchScalarGridSpec(
            num_scalar_prefetch=2,
            grid=(nf, ntiles),
            in_specs=[pl.BlockSpec((MOE_TM, D // 2), lambda f, t, te, nu: (tile(t, nu), 0)),
                      wspec(0), wspec(nf), bspec(0), bspec(nf)],
            out_specs=pl.BlockSpec((MOE_TM, tf), lambda f, t, te, nu: (t, f)),
            scratch_shapes=[pltpu.VMEM((D, tf), BF16), pltpu.VMEM((D, tf), BF16)]),
        out_shape=jax.ShapeDtypeStruct((p_rows, DFF), BF16),
        compiler_params=_cp(2), name="moe_gate_up",
    )(texp, nused, xs, w_gate_up, w_gate_up, bgu, bgu)


def _down_body(texp_ref, nused_ref, a_ref, w_ref, b_ref, y_ref, w_bf):
    t = pl.program_id(1)

    @pl.when(t < nused_ref[0])
    def _():
        @pl.when(_expert_changed(texp_ref, t))
        def _():
            w_bf[...] = w_ref[...].astype(BF16)

        y_ref[...] = jnp.dot(a_ref[...], w_bf[...], preferred_element_type=F32) + b_ref[...]

    @pl.when(t >= nused_ref[0])
    def _():
        y_ref[...] = jnp.zeros_like(y_ref)


def _down(act, texp, nused, w_down, b_down, layer):
    p_rows = act.shape[0]
    ntiles = p_rows // MOE_TM
    tn = 1024
    tile = lambda t, nu: jnp.minimum(t, nu[0] - 1)
    bd = b_down.reshape(N_LAYERS, N_EXP, 1, D)
    return pl.pallas_call(
        _down_body,
        grid_spec=pltpu.PrefetchScalarGridSpec(
            num_scalar_prefetch=2,
            grid=(D // tn, ntiles),
            in_specs=[pl.BlockSpec((MOE_TM, DFF), lambda n, t, te, nu: (tile(t, nu), 0)),
                      pl.BlockSpec((None, None, DFF, tn), lambda n, t, te, nu: (layer, te[tile(t, nu)], 0, n)),
                      pl.BlockSpec((None, None, 1, tn), lambda n, t, te, nu: (layer, te[tile(t, nu)], 0, n))],
            out_specs=pl.BlockSpec((MOE_TM, tn), lambda n, t, te, nu: (t, n)),
            scratch_shapes=[pltpu.VMEM((DFF, tn), BF16)]),
        out_shape=jax.ShapeDtypeStruct((p_rows, D), F32),
        compiler_params=_cp(2), name="moe_down",
    )(texp, nused, act, w_down, bd)


def _combine_body(dest_hbm, ys_hbm, x_ref, wt_ref, gp_ref, gs_ref, o_ref, dsm, buf, sem_s, sem_r, *, ns):
    i = pl.program_id(0)
    slot = _route_chunk(dest_hbm, dsm, sem_s, i, pl.num_programs(0))

    def row_copy(r, k, d):
        return pltpu.make_async_copy(ys_hbm.at[pl.ds(d, 1), :], buf.at[k, pl.ds(r, 1), :], sem_r)

    def combine(rows, gate_ref):
        def issue(r, c):
            for k in range(TOPK):
                row_copy(r, k, dsm[slot + r * TOPK + k]).start()
            return c

        lax.fori_loop(0, rows, issue, 0)

        def drain(r, c):
            for k in range(TOPK):
                row_copy(0, 0, 0).wait()
            return c

        lax.fori_loop(0, rows, drain, 0)
        wt = wt_ref[:rows]
        moe = buf[0, :rows] * wt[:, 0:1]
        for k in range(1, TOPK):
            moe = moe + buf[k, :rows] * wt[:, k:k + 1]
        o_ref[:rows] = x_ref[:rows] + gate_ref[...] * moe

    def prompt():
        combine(x_ref.shape[0], gp_ref)

    def sample():
        combine(ns, gs_ref)
        o_ref[ns:] = jnp.zeros_like(o_ref[ns:])

    _prompt_or_sample(i, pl.num_programs(0), prompt, sample)


def _combine(x_all, ys, dest_pad, wts, mods, layer, np_rows, ns):
    nt = x_all.shape[0]
    assert np_rows % TOK_CHUNK == 0 and ns <= TOK_CHUNK
    any_spec = pl.BlockSpec(memory_space=pl.ANY)
    gp, gs = mods.specs(layer, 2, TOK_CHUNK)
    return pl.pallas_call(
        functools.partial(_combine_body, ns=ns),
        grid=(np_rows // TOK_CHUNK + 1,),
        in_specs=[any_spec, any_spec,
                  pl.BlockSpec((TOK_CHUNK, D), lambda i: (i, 0)),
                  pl.BlockSpec((TOK_CHUNK, TOPK), lambda i: (i, 0)),
                  gp, gs],
        out_specs=pl.BlockSpec((TOK_CHUNK, D), lambda i: (i, 0)),
        out_shape=jax.ShapeDtypeStruct((nt, D), F32),
        scratch_shapes=[pltpu.SMEM((2 * ROUTE_CHUNK,), I32), pltpu.VMEM((TOPK, TOK_CHUNK, D), F32),
                        pltpu.SemaphoreType.DMA((2,)), pltpu.SemaphoreType.DMA(())],
        compiler_params=_cp(1), name="moe_combine",
    )(dest_pad, ys, x_all, wts, *mods.args)


def _moe(x_all, g, mods, layer, router_w, router_b, w_gate_up, b_gate_up, w_down, b_down, np_rows, ns):
    nt = x_all.shape[0]
    hp, idx, wts, rank, cnt = _route(x_all, g, mods, layer, router_w, router_b, np_rows, ns)
    counts = cnt[0, :N_EXP].astype(I32)
    padded = ((counts + MOE_TM - 1) // MOE_TM) * MOE_TM
    ends = jnp.cumsum(padded)
    gstart = ends - padded
    dest = gstart[idx] + rank
    n_entries = nt * TOPK
    n_chunks = -(-n_entries // ROUTE_CHUNK)
    dest_pad = jnp.zeros((n_chunks * ROUTE_CHUNK,), I32).at[:n_entries].set(dest.reshape(-1))
    ntiles = -(-n_entries // MOE_TM) + N_EXP
    tile_ends = ends // MOE_TM
    tile_ids = jnp.arange(ntiles, dtype=I32)
    texp = jnp.minimum(jnp.sum((tile_ids[:, None] >= tile_ends[None, :]).astype(I32), axis=1), N_EXP - 1)
    nused = tile_ends[-1:].astype(I32)
    last_tile = jnp.where(padded > 0, tile_ends - 1, -1).astype(I32)
    xs = _dispatch(hp, dest_pad, last_tile, nused, ntiles)
    act = _gate_up(xs, texp, nused, w_gate_up, b_gate_up, layer)
    ys = _down(act, texp, nused, w_down, b_down, layer)
    return _combine(x_all, ys, dest_pad, wts, mods, layer, np_rows, ns)


def _final_norm_body(x_ref, g_ref, o_ref):
    o_ref[...] = _normed(x_ref[...], g_ref[...])


def _final_norm(x_all, g, np_rows, ns):
    tr = 1024
    gspec = pl.BlockSpec((1, D), lambda i: (0, 0))
    yp = pl.pallas_call(
        _final_norm_body, grid=(np_rows // tr,),
        in_specs=[pl.BlockSpec((tr, D), lambda i: (i, 0)), gspec],
        out_specs=pl.BlockSpec((tr, D), lambda i: (i, 0)),
        out_shape=jax.ShapeDtypeStruct((np_rows, D), F32),
        compiler_params=_cp(1), name="final_norm_prompt",
    )(x_all, g)
    sblk = np_rows // ns
    ysm = pl.pallas_call(
        _final_norm_body, grid=(1,),
        in_specs=[pl.BlockSpec((ns, D), lambda i: (sblk, 0)), gspec],
        out_specs=pl.BlockSpec((ns, D), lambda i: (0, 0)),
        out_shape=jax.ShapeDtypeStruct((ns, D), F32),
        compiler_params=_cp(1), name="final_norm_sample",
    )(x_all, g)
    return yp, ysm


def kernel(x_prompt, x_sample, state_ret, cache_kv_w128_d1, cache_kv_w512_d4, cache_kv_w2048_d16, c_prompt, c_sample, norm1_g, ada1_w, ada1_b, w_in, w_out_ret, w_out_dil, w_out, norm2_g, ada2_w, ada2_b, router_w, router_b, w_gate_up, b_gate_up, w_down, b_down, final_norm_g):
    nb, seq, _ = x_prompt.shape
    ns, dec_seq, _ = x_sample.shape
    assert dec_seq == 1 and seq == 2048 and ns % 16 == 0
    past = cache_kv_w2048_d16.shape[2]
    np_rows = nb * seq
    assert np_rows % ns == 0
    caches = (cache_kv_w128_d1, cache_kv_w512_d4, cache_kv_w2048_d16)

    x_src = (x_prompt.reshape(np_rows, D), x_sample.reshape(ns, D), 0)
    nc = -(-(ns + nb) // 16) * 16
    c_all = jnp.zeros((nc, D), F32).at[:ns].set(c_sample).at[ns:ns + nb].set(c_prompt)
    mods1 = _Mods(_mods(c_all, ada1_w, ada1_b), ns, nb, seq)
    mods2 = _Mods(_mods(c_all, ada2_w, ada2_b), ns, nb, seq)
    g1 = norm1_g.reshape(N_LAYERS, 1, D)
    g2 = norm2_g.reshape(N_LAYERS, 1, D)

    ret_p, ret_s, projs, kv_s = [], [], [], [[] for _ in DIL]
    for layer in range(N_LAYERS):
        h = _norm_mod(x_src, g1, mods1, layer, np_rows, ns)
        proj = _proj(h, w_in, layer)
        projs.append(proj)
        ya_p, st_p = _ret_prompt(proj, nb, seq)
        ya_s, st_s = _ret_sample(proj, state_ret, layer, np_rows, ns, past)
        yb_p = _dil_prompt(proj, nb, seq)
        yb_s = _dil_sample(proj, caches, layer, np_rows, ns)
        x_all = _mix_out(x_src, ya_p, yb_p, ya_s, yb_s, proj, w_out_ret, w_out_dil, w_out, mods1, layer)
        x_all = _moe(x_all, g2, mods2, layer, router_w, router_b, w_gate_up, b_gate_up, w_down, b_down,
                     np_rows, ns)
        x_src = (x_all, x_all, np_rows // ns)
        ret_p.append(st_p)
        ret_s.append(st_s)
        for g in range(len(DIL)):
            c0 = OFF_DIL + g * 3 * DIL_W + DIL_W
            kv_s[g].append(proj[np_rows:, c0:c0 + 2 * DIL_W].reshape(ns, 1, 2, DIL_H, DIL_E))
    y_p, y_s = _final_norm(x_all, final_norm_g.reshape(1, D), np_rows, ns)
    kv_p = [_kv_out(projs, g, nb, seq) for g in range(len(DIL))]
    return (y_p.reshape(nb, seq, D), y_s.reshape(ns, 1, D),
            jnp.stack(ret_p), kv_p[0], kv_p[1], kv_p[2],
            jnp.stack(ret_s), jnp.stack(kv_s[0]), jnp.stack(kv_s[1]), jnp.stack(kv_s[2]))
```

```python
import functools

import jax
import jax.numpy as jnp
from jax import lax
from jax.experimental import pallas as pl
from jax.experimental.pallas import tpu as pltpu

F32 = jnp.float32
BF16 = jnp.bfloat16
U32 = jnp.uint32
I32 = jnp.int32

D = 2048
N_LAYERS = 2
RET_H = 8
RET_DK = 128
RET_DV = 256
RET_C = 128
ROPE_BASE = 10000.0
DIL = ((128, 1), (512, 4), (2048, 16))
DIL_H = 4
DIL_E = 128
DIL_W = DIL_H * DIL_E
N_EXP = 32
TOPK = 4
DFF = 2048
SW_LIMIT = 7.0
SW_ALPHA = 1.702
EPS = 1e-6

OFF_Q = 0
OFF_K = 1024
OFF_V = 2048
OFF_G = 4096
OFF_DIL = 6144
OFF_GA = 10752
OFF_GB = 12800
IN_W = 14848

VMEM_LIMIT = 56 * 1024 * 1024
MOE_TM = 512
DOWN_TN = 2048
ROUTE_CHUNK = 1024
TOK_CHUNK = ROUTE_CHUNK // TOPK


def _cp(n_grid_dims):
    return pltpu.CompilerParams(
        dimension_semantics=("arbitrary",) * n_grid_dims,
        vmem_limit_bytes=VMEM_LIMIT)


def _row_tile(n, cap):
    for k in range(1, n + 1):
        if n % k == 0 and n // k <= cap and (n // k) % 16 == 0:
            return n // k
    raise ValueError((n, cap))


def _silu(x):
    return x * jax.nn.sigmoid(x)


def _mods_body(c_ref, w_ref, b_ref, o_ref):
    a = _silu(c_ref[...]).astype(BF16)
    o_ref[...] = jnp.dot(a, w_ref[...].astype(BF16), preferred_element_type=F32) + b_ref[...]


def _mods(c_all, ada_w, ada_b):
    nc = c_all.shape[0]
    tn = 1024
    return pl.pallas_call(
        _mods_body,
        grid=(N_LAYERS, 3 * D // tn),
        in_specs=[pl.BlockSpec((nc, D), lambda l, n: (0, 0)),
                  pl.BlockSpec((None, D, tn), lambda l, n: (l, 0, n)),
                  pl.BlockSpec((None, 1, tn), lambda l, n: (l, 0, n))],
        out_specs=pl.BlockSpec((None, nc, tn), lambda l, n: (l, 0, n)),
        out_shape=jax.ShapeDtypeStruct((N_LAYERS, nc, 3 * D), F32),
        compiler_params=_cp(2), name="ada_mods",
    )(c_all, ada_w, ada_b.reshape(N_LAYERS, 1, 3 * D))


class _Mods:
    def __init__(self, m, ns, nb, seq):
        self.m3 = m
        self.m4 = m.reshape(m.shape[0], m.shape[1], 1, 3 * D)
        self.ns, self.nb, self.seq = ns, nb, seq

    @property
    def args(self):
        return (self.m4, self.m3)

    def specs(self, layer, which, tr, tn=D, col_axis=False):
        per = self.seq // tr
        ns, nb = self.ns, self.nb
        nblk = D // tn
        if col_axis:
            p_map = lambda n, i: (layer, ns + jnp.minimum(i // per, nb - 1), 0, which * nblk + n)
            s_map = lambda n, i: (layer, 0, which * nblk + n)
        else:
            p_map = lambda i: (layer, ns + jnp.minimum(i // per, nb - 1), 0, which * nblk)
            s_map = lambda i: (layer, 0, which * nblk)
        return [pl.BlockSpec((None, None, 1, tn), p_map), pl.BlockSpec((None, ns, tn), s_map)]


def _prompt_or_sample(step, n_steps, prompt_fn, sample_fn):
    pl.when(step < n_steps - 1)(prompt_fn)
    pl.when(step == n_steps - 1)(sample_fn)


def _normed(x, g):
    return x * lax.rsqrt(jnp.mean(x * x, axis=-1, keepdims=True) + EPS) * g


def _norm_mod_body(xp_ref, xs_ref, g_ref, shp_ref, scp_ref, shs_ref, scs_ref, o_ref, *, ns):
    def prompt():
        y = _normed(xp_ref[...], g_ref[...])
        o_ref[...] = (y * (1.0 + scp_ref[...]) + shp_ref[...]).astype(o_ref.dtype)

    def sample():
        y = _normed(xs_ref[...], g_ref[...])
        o_ref[:ns] = (y * (1.0 + scs_ref[...]) + shs_ref[...]).astype(o_ref.dtype)
        o_ref[ns:] = jnp.zeros_like(o_ref[ns:])

    _prompt_or_sample(pl.program_id(0), pl.num_programs(0), prompt, sample)


def _norm_mod(x_src, g, mods, layer, np_rows, ns):
    x_main, x_samp, sblk = x_src
    tr = 1024
    npt = np_rows // tr
    shp, shs = mods.specs(layer, 0, tr)
    scp, scs = mods.specs(layer, 1, tr)
    return pl.pallas_call(
        functools.partial(_norm_mod_body, ns=ns),
        grid=(npt + 1,),
        in_specs=[pl.BlockSpec((tr, D), lambda i: (jnp.minimum(i, npt - 1), 0)),
                  pl.BlockSpec((ns, D), lambda i: (sblk, 0)),
                  pl.BlockSpec((None, 1, D), lambda i: (layer, 0, 0)),
                  shp, scp, shs, scs],
        out_specs=pl.BlockSpec((tr, D), lambda i: (i, 0)),
        out_shape=jax.ShapeDtypeStruct((np_rows + ns, D), BF16),
        compiler_params=_cp(1), name="norm1",
    )(x_main, x_samp, g, mods.m4, mods.m4, mods.m3, mods.m3)


def _proj_body(x_ref, w_ref, o_ref, wbf_ref):
    @pl.when(pl.program_id(1) == 0)
    def _():
        wbf_ref[...] = w_ref[...].astype(BF16)

    o_ref[...] = jnp.dot(x_ref[...], wbf_ref[...], preferred_element_type=F32).astype(o_ref.dtype)


def _proj(h, w_in, layer):
    nt = h.shape[0]
    tm = _row_tile(nt, 1376)
    tn = 1024
    return pl.pallas_call(
        _proj_body,
        grid=(pl.cdiv(IN_W, tn), nt // tm),
        in_specs=[pl.BlockSpec((tm, D), lambda n, m: (m, 0)),
                  pl.BlockSpec((None, D, tn), lambda n, m: (layer, 0, n))],
        out_specs=pl.BlockSpec((tm, tn), lambda n, m: (m, n)),
        out_shape=jax.ShapeDtypeStruct((nt, IN_W), F32),
        scratch_shapes=[pltpu.VMEM((D, tn), BF16)],
        compiler_params=_cp(2), name="in_proj",
    )(h, w_in)


def _rope_tables(pos):
    half = RET_DK // 2
    inv = ROPE_BASE ** -jnp.linspace(0.0, 1.0, half, dtype=F32)
    ang = pos.astype(F32)[:, None] * inv[None, :]
    cos = jnp.cos(ang)
    sin = jnp.sin(ang)
    return jnp.concatenate([cos, cos], axis=-1), jnp.concatenate([-sin, sin], axis=-1)


def _decay_tables(chunk):
    log_g = jnp.log1p(-jnp.exp2(-5.0 - jnp.arange(RET_H, dtype=F32)))
    idx = jnp.arange(chunk, dtype=F32)
    rel = idx[:, None] - idx[None, :]
    inner = jnp.where(rel >= 0, jnp.exp(log_g[:, None, None] * jnp.maximum(rel, 0.0)), 0.0)
    q_decay = jnp.exp(log_g[:, None] * (idx + 1.0)[None, :])
    k_decay = jnp.exp(log_g[:, None] * (chunk - 1.0 - idx)[None, :])
    chunk_decay = jnp.exp(log_g * chunk)
    return inner, q_decay, k_decay, chunk_decay


def _rot(x, cos, sin):
    return x * cos + pltpu.roll(x, RET_DK // 2, 1) * sin


def _head_norm_gate(o, g):
    on = o * lax.rsqrt(jnp.mean(o * o, axis=-1, keepdims=True) + EPS)
    return _silu(g) * on


def _ret_prompt_body(q_ref, k_ref, v_ref, g_ref, cos_ref, sin_ref, inner_ref, qd_ref, kd_ref, cd_ref,
                     ya_ref, s_ref):
    @pl.when(pl.program_id(1) == 0)
    def _():
        s_ref[...] = jnp.zeros_like(s_ref)

    cos = cos_ref[...]
    sin = sin_ref[...]
    for h in range(RET_H):
        q = _rot(q_ref[:, h * RET_DK:(h + 1) * RET_DK], cos, sin)
        k = _rot(k_ref[:, h * RET_DK:(h + 1) * RET_DK], cos, sin) * (RET_DK ** -0.5)
        vb = v_ref[:, h * RET_DV:(h + 1) * RET_DV].astype(BF16)
        s = lax.dot_general(q.astype(BF16), k.astype(BF16), (((1,), (1,)), ((), ())),
                            preferred_element_type=F32) * inner_ref[h]
        state = s_ref[h]
        o = (jnp.dot(s.astype(BF16), vb, preferred_element_type=F32)
             + jnp.dot((q * qd_ref[h]).astype(BF16), state.astype(BF16), preferred_element_type=F32))
        kt = (k * kd_ref[h]).T.astype(BF16)
        s_ref[h] = state * cd_ref[h] + jnp.dot(kt, vb, preferred_element_type=F32)
        y = _head_norm_gate(o, g_ref[:, h * RET_DV:(h + 1) * RET_DV])
        ya_ref[:, h * RET_DV:(h + 1) * RET_DV] = y.astype(ya_ref.dtype)


def _ret_prompt(proj, nb, seq):
    nch = seq // RET_C
    cos, sin = _rope_tables(jnp.arange(seq, dtype=I32))
    inner, qd, kd, cd = _decay_tables(RET_C)
    qd_b = jnp.broadcast_to(qd[:, :, None], (RET_H, RET_C, RET_DK))
    kd_b = jnp.broadcast_to(kd[:, :, None], (RET_H, RET_C, RET_DK))
    cd_b = jnp.broadcast_to(cd[:, None, None], (RET_H, 1, RET_DV))
    row = lambda b, c: b * nch + c
    full3 = lambda shp: pl.BlockSpec(shp, lambda b, c: (0, 0, 0))
    return pl.pallas_call(
        _ret_prompt_body,
        grid=(nb, nch),
        in_specs=[pl.BlockSpec((RET_C, 1024), lambda b, c: (row(b, c), OFF_Q // 1024)),
                  pl.BlockSpec((RET_C, 1024), lambda b, c: (row(b, c), OFF_K // 1024)),
                  pl.BlockSpec((RET_C, 2048), lambda b, c: (row(b, c), OFF_V // 2048)),
                  pl.BlockSpec((RET_C, 2048), lambda b, c: (row(b, c), OFF_G // 2048)),
                  pl.BlockSpec((RET_C, RET_DK), lambda b, c: (c, 0)),
                  pl.BlockSpec((RET_C, RET_DK), lambda b, c: (c, 0)),
                  full3((RET_H, RET_C, RET_C)), full3((RET_H, RET_C, RET_DK)),
                  full3((RET_H, RET_C, RET_DK)), full3((RET_H, 1, RET_DV))],
        out_specs=[pl.BlockSpec((RET_C, D), lambda b, c: (row(b, c), 0)),
                   pl.BlockSpec((None, RET_H, RET_DK, RET_DV), lambda b, c: (b, 0, 0, 0))],
        out_shape=[jax.ShapeDtypeStruct((nb * seq, D), BF16),
                   jax.ShapeDtypeStruct((nb, RET_H, RET_DK, RET_DV), F32)],
        compiler_params=_cp(2), name="retention_prompt",
    )(proj, proj, proj, proj, cos, sin, inner, qd_b, kd_b, cd_b)


RS_NB = 8


def _ret_sample_body(q_ref, k_ref, v_ref, g_ref, cos_ref, sin_ref, inner_ref, qd_ref, kd_ref, cd_ref,
                     st_ref, ya_ref, so_ref):
    i = pl.program_id(0)
    cos = cos_ref[...]
    sin = sin_ref[...]
    row_i = lax.broadcasted_iota(I32, (RET_DK, RET_DK), 0)
    col_i = lax.broadcasted_iota(I32, (RET_DK, RET_DK), 1)
    eye = row_i == col_i
    rows = pl.ds(pl.multiple_of(i * RS_NB, RS_NB), RS_NB)
    q_all = q_ref[rows, :]
    k_all = k_ref[rows, :]
    v_all = v_ref[rows, :]
    g_all = g_ref[rows, :]
    out_rows = []
    for j in range(RS_NB):
        heads = []
        for h in range(RET_H):
            q = _rot(q_all[j:j + 1, h * RET_DK:(h + 1) * RET_DK], cos, sin)
            k = _rot(k_all[j:j + 1, h * RET_DK:(h + 1) * RET_DK], cos, sin) * (RET_DK ** -0.5)
            v = v_all[j:j + 1, h * RET_DV:(h + 1) * RET_DV]
            s = jnp.sum(q * k, axis=-1, keepdims=True) * inner_ref[h]
            state = st_ref[j, h]
            q8 = jnp.broadcast_to(q * qd_ref[h], (8, RET_DK)).astype(BF16)
            qs = jnp.dot(q8, state.astype(BF16), preferred_element_type=F32)[0:1]
            o = s * v + qs
            kdiag = jnp.where(eye, jnp.broadcast_to(k * kd_ref[h], (RET_DK, RET_DK)), 0.0).astype(BF16)
            vb = jnp.broadcast_to(v, (RET_DK, RET_DV)).astype(BF16)
            so_ref[j, h] = state * cd_ref[h] + jnp.dot(kdiag, vb, preferred_element_type=F32)
            heads.append(_head_norm_gate(o, g_all[j:j + 1, h * RET_DV:(h + 1) * RET_DV]))
        out_rows.append(jnp.concatenate(heads, axis=1))
    ya_ref[...] = jnp.concatenate(out_rows, axis=0)


def _ret_sample(proj, state_ret, layer, np_rows, ns, past):
    cos, sin = _rope_tables(past + jnp.arange(1, dtype=I32))
    inner, qd, kd, cd = _decay_tables(1)
    inner_b = inner.reshape(RET_H, 1, 1)
    qd_b = jnp.broadcast_to(qd[:, :, None], (RET_H, 1, RET_DK))
    kd_b = jnp.broadcast_to(kd[:, :, None], (RET_H, 1, RET_DK))
    cd_b = jnp.broadcast_to(cd[:, None, None], (RET_H, 1, RET_DV))
    sblk = np_rows // ns
    full = lambda shp: pl.BlockSpec(shp, lambda i: (0,) * len(shp))
    return pl.pallas_call(
        _ret_sample_body,
        grid=(ns // RS_NB,),
        in_specs=[pl.BlockSpec((ns, 1024), lambda i: (sblk, OFF_Q // 1024)),
                  pl.BlockSpec((ns, 1024), lambda i: (sblk, OFF_K // 1024)),
                  pl.BlockSpec((ns, 2048), lambda i: (sblk, OFF_V // 2048)),
                  pl.BlockSpec((ns, 2048), lambda i: (sblk, OFF_G // 2048)),
                  full((1, RET_DK)), full((1, RET_DK)),
                  full((RET_H, 1, 1)), full((RET_H, 1, RET_DK)), full((RET_H, 1, RET_DK)),
                  full((RET_H, 1, RET_DV)),
                  pl.BlockSpec((None, RS_NB, RET_H, RET_DK, RET_DV), lambda i: (layer, i, 0, 0, 0))],
        out_specs=[pl.BlockSpec((RS_NB, D), lambda i: (i, 0)),
                   pl.BlockSpec((RS_NB, RET_H, RET_DK, RET_DV), lambda i: (i, 0, 0, 0))],
        out_shape=[jax.ShapeDtypeStruct((ns, D), F32),
                   jax.ShapeDtypeStruct((ns, RET_H, RET_DK, RET_DV), F32)],
        compiler_params=_cp(1), name="retention_sample",
    )(proj, proj, proj, proj, cos, sin, inner_b, qd_b, kd_b, cd_b, state_ret)


def _merge_groups(parts):
    m_all = parts[0][1]
    for _, m_g, _ in parts[1:]:
        m_all = jnp.maximum(m_all, m_g)
    num = 0.0
    den = 0.0
    for n_g, m_g, l_g in parts:
        w = jnp.exp(m_g - m_all)
        num = num + n_g * w
        den = den + l_g * w
    return num / den


def _dil_prompt_body(*refs, seq):
    qkv = refs[:9]
    yb_ref = refs[9]
    num_ref, m_ref, l_ref = refs[10:13]
    band = 128
    scale = DIL_E ** -0.5
    qi = lax.broadcasted_iota(I32, (band, band), 0)
    kj1 = lax.broadcasted_iota(I32, (band, band), 1)
    qi2 = lax.broadcasted_iota(I32, (band, 2 * band), 0)
    kj2 = lax.broadcasted_iota(I32, (band, 2 * band), 1)
    first_valid = kj1 <= qi
    later_valid = (kj2 >= qi2) & (kj2 <= qi2 + band)
    for g, (window, dil) in enumerate(DIL):
        assert window // dil == band
        q_ref, k_ref, v_ref = qkv[3 * g:3 * g + 3]
        nblk = seq // dil // band
        for p in range(dil):
            for n in range(nblk):
                rows_q = pl.ds(p + n * band * dil, band, stride=dil)
                qb = q_ref[rows_q, :].astype(BF16)
                if n == 0:
                    rows_k = rows_q
                    valid = first_valid
                else:
                    rows_k = pl.ds(p + (n - 1) * band * dil, 2 * band, stride=dil)
                    valid = later_valid
                kk = k_ref[rows_k, :].astype(BF16)
                vv = v_ref[rows_k, :].astype(BF16)
                s = lax.dot_general(qb, kk, (((1,), (1,)), ((), ())), preferred_element_type=F32) * scale
                s = jnp.where(valid, s, -jnp.inf)
                m = jnp.max(s, axis=-1, keepdims=True)
                pe = jnp.exp(s - m)
                l = jnp.sum(pe, axis=-1, keepdims=True)
                num_ref[g, rows_q, :] = jnp.dot(pe.astype(BF16), vv, preferred_element_type=F32)
                m_ref[g, rows_q, :] = jnp.broadcast_to(m, (band, DIL_E))
                l_ref[g, rows_q, :] = jnp.broadcast_to(l, (band, DIL_E))
    parts = [(num_ref[g], m_ref[g], l_ref[g]) for g in range(len(DIL))]
    yb_ref[...] = _merge_groups(parts).astype(yb_ref.dtype)


def _dil_prompt(proj, nb, seq):
    in_specs = []
    for g in range(len(DIL)):
        for j in range(3):
            cb = (OFF_DIL + g * 3 * DIL_W + j * DIL_W) // DIL_E
            in_specs.append(pl.BlockSpec((seq, DIL_E), lambda b, h, cb=cb: (b, cb + h)))
    scr = pltpu.VMEM((len(DIL), seq, DIL_E), F32)
    return pl.pallas_call(
        functools.partial(_dil_prompt_body, seq=seq),
        grid=(nb, DIL_H),
        in_specs=in_specs,
        out_specs=pl.BlockSpec((seq, DIL_E), lambda b, h: (b, h)),
        out_shape=jax.ShapeDtypeStruct((nb * seq, DIL_W), BF16),
        scratch_shapes=[scr, scr, scr],
        compiler_params=_cp(2), name="dilated_prompt",
    )(*([proj] * 9))


DS_NB = 8


def _dil_sample_body(c0_ref, c1_ref, c2_ref, q0_ref, q1_ref, q2_ref, yb_ref):
    i = pl.program_id(0)
    caches = (c0_ref, c1_ref, c2_ref)
    scale = DIL_E ** -0.5
    rows = pl.ds(pl.multiple_of(i * DS_NB, DS_NB), DS_NB)
    news = [ref[rows, :] for ref in (q0_ref, q1_ref, q2_ref)]
    out_rows = []
    for j in range(DS_NB):
        parts = []
        for g in range(len(DIL)):
            new = news[g][j:j + 1]
            q4 = jnp.concatenate([new[:, h * DIL_E:(h + 1) * DIL_E] for h in range(DIL_H)], axis=0)
            k4 = jnp.concatenate([new[:, DIL_W + h * DIL_E:DIL_W + (h + 1) * DIL_E] for h in range(DIL_H)], axis=0)
            v4 = jnp.concatenate([new[:, 2 * DIL_W + h * DIL_E:2 * DIL_W + (h + 1) * DIL_E] for h in range(DIL_H)],
                                 axis=0)
            kc = caches[g][j, :, 0:DIL_H, :]
            vc = caches[g][j, :, DIL_H:2 * DIL_H, :]
            s_c = jnp.sum(kc * q4[None], axis=-1, keepdims=True) * scale
            s_n = jnp.sum(k4 * q4, axis=-1, keepdims=True) * scale
            m = jnp.maximum(jnp.max(s_c, axis=0), s_n)
            p_c = jnp.exp(s_c - m[None])
            p_n = jnp.exp(s_n - m)
            l = jnp.sum(p_c, axis=0) + p_n
            num = jnp.sum(p_c * vc, axis=0) + p_n * v4
            parts.append((num, m, l))
        y = _merge_groups(parts)
        out_rows.append(jnp.concatenate([y[h:h + 1] for h in range(DIL_H)], axis=1))
    yb_ref[...] = jnp.concatenate(out_rows, axis=0)


def _dil_sample(proj, caches, layer, np_rows, ns):
    sblk = np_rows // ns
    views = []
    in_specs = []
    for cache, (window, dil) in zip(caches, DIL):
        wb = cache.shape[2]
        assert wb == window and wb % dil == 0
        views.append(cache.reshape(cache.shape[0] * ns, wb // dil, dil, 2 * DIL_H, DIL_E))
        in_specs.append(pl.BlockSpec((DS_NB, wb // dil, None, 2 * DIL_H, DIL_E),
                                     lambda i: (layer * (ns // DS_NB) + i, 0, 0, 0, 0)))
    for g in range(len(DIL)):
        in_specs.append(pl.BlockSpec((ns, 3 * DIL_W), lambda i, g=g: (sblk, OFF_DIL // (3 * DIL_W) + g)))
    return pl.pallas_call(
        _dil_sample_body,
        grid=(ns // DS_NB,),
        in_specs=in_specs,
        out_specs=pl.BlockSpec((DS_NB, DIL_W), lambda i: (i, 0)),
        out_shape=jax.ShapeDtypeStruct((ns, DIL_W), F32),
        compiler_params=_cp(1), name="dilated_sample",
    )(*views, proj, proj, proj)


def _kv_out_body(*refs):
    o_ref = refs[-1]
    layer = pl.program_id(0)
    for l in range(N_LAYERS):
        @pl.when(layer == l)
        def _(k_ref=refs[2 * l], v_ref=refs[2 * l + 1]):
            for h in range(DIL_H):
                o_ref[:, 0, h, :] = k_ref[:, h * DIL_E:(h + 1) * DIL_E]
                o_ref[:, 1, h, :] = v_ref[:, h * DIL_E:(h + 1) * DIL_E]


def _kv_out(projs, g, nb, seq):
    window = min(DIL[g][0], seq)
    rows = min(window, 512)
    nr = window // rows
    cb = (OFF_DIL + g * 3 * DIL_W + DIL_W) // DIL_W
    first = (seq - window) // rows
    per_b = seq // rows
    last = (nb - 1) * per_b + first + nr - 1
    in_specs = []
    for l in range(N_LAYERS):
        def rblk(ll, b, r, l=l):
            cur = b * per_b + first + r
            return jnp.where(ll == l, cur, jnp.where(ll < l, first, last))
        in_specs.append(pl.BlockSpec((rows, DIL_W), lambda ll, b, r, f=rblk: (f(ll, b, r), cb)))
        in_specs.append(pl.BlockSpec((rows, DIL_W), lambda ll, b, r, f=rblk: (f(ll, b, r), cb + 1)))
    args = []
    for p in projs:
        args += [p, p]
    return pl.pallas_call(
        _kv_out_body,
        grid=(N_LAYERS, nb, nr),
        in_specs=in_specs,
        out_specs=pl.BlockSpec((None, None, rows, 2, DIL_H, DIL_E), lambda ll, b, r: (ll, b, r, 0, 0, 0)),
        out_shape=jax.ShapeDtypeStruct((N_LAYERS, nb, window, 2, DIL_H, DIL_E), F32),
        compiler_params=_cp(3), name="kv_prompt_out",
    )(*args)


W_CHUNK = 256


def _load_weight_bf16(w_hbm, layer, dst_bf, stage, sems):
    n = dst_bf.shape[0] // W_CHUNK

    def chunk(c):
        return pltpu.make_async_copy(w_hbm.at[layer, pl.ds(c * W_CHUNK, W_CHUNK), :], stage.at[c % 2],
                                     sems.at[c % 2])

    chunk(0).start()
    for c in range(n):
        if c + 1 < n:
            chunk(c + 1).start()
        chunk(c).wait()
        dst_bf[c * W_CHUNK:(c + 1) * W_CHUNK, :] = stage[c % 2].astype(BF16)


GATE_TN = 512


def _mix_out_body(xp_ref, xs_ref, yap_ref, ybp_ref, yas_ref, ybs_ref, *rest, ns, layer):
    n_gate = D // GATE_TN
    ga_refs, gb_refs = rest[:n_gate], rest[n_gate:2 * n_gate]
    gp_ref, gs_ref, wr_hbm, wd_hbm, wo_hbm, o_ref, wr_bf, wd_bf, wo_bf, stage, sems = rest[2 * n_gate:]

    @pl.when(pl.program_id(0) == 0)
    def _():
        _load_weight_bf16(wr_hbm, layer, wr_bf, stage, sems)
        _load_weight_bf16(wd_hbm, layer, wd_bf, stage, sems)
        _load_weight_bf16(wo_hbm, layer, wo_bf, stage, sems)

    def mixed(rows, ya, yb, x, gate):
        parts = []
        for c in range(n_gate):
            cols = slice(c * GATE_TN, (c + 1) * GATE_TN)
            y_a = jnp.dot(ya, wr_bf[:, cols], preferred_element_type=F32)
            y_b = jnp.dot(yb, wd_bf[:, cols], preferred_element_type=F32)
            merged = jax.nn.sigmoid(ga_refs[c][:rows]) * y_a + jax.nn.sigmoid(gb_refs[c][:rows]) * y_b
            parts.append(merged.astype(BF16))
        mix = jnp.dot(jnp.concatenate(parts, axis=1), wo_bf[...], preferred_element_type=F32)
        return x + gate * mix

    def prompt():
        o_ref[...] = mixed(o_ref.shape[0], yap_ref[...], ybp_ref[...], xp_ref[...], gp_ref[...])

    def sample():
        o_ref[:ns] = mixed(ns, yas_ref[...].astype(BF16), ybs_ref[...].astype(BF16), xs_ref[...], gs_ref[...])
        o_ref[ns:] = jnp.zeros_like(o_ref[ns:])

    _prompt_or_sample(pl.program_id(0), pl.num_programs(0), prompt, sample)


def _mix_out(x_src, ya_p, yb_p, ya_s, yb_s, proj, w_out_ret, w_out_dil, w_out, mods, layer):
    x_main, x_samp, sblk = x_src
    nt = proj.shape[0]
    np_rows = ya_p.shape[0]
    ns = ya_s.shape[0]
    tm = 256
    npt = np_rows // tm
    ptile = lambda i: jnp.minimum(i, npt - 1)
    gp, gs = mods.specs(layer, 2, tm)
    any_spec = pl.BlockSpec(memory_space=pl.ANY)
    gate_specs = [pl.BlockSpec((tm, GATE_TN), lambda i, cb=(off // GATE_TN + c): (i, cb))
                  for off in (OFF_GA, OFF_GB) for c in range(D // GATE_TN)]
    return pl.pallas_call(
        functools.partial(_mix_out_body, ns=ns, layer=layer),
        grid=(npt + 1,),
        in_specs=[pl.BlockSpec((tm, D), lambda i: (ptile(i), 0)),
                  pl.BlockSpec((ns, D), lambda i: (sblk, 0)),
                  pl.BlockSpec((tm, D), lambda i: (ptile(i), 0)),
                  pl.BlockSpec((tm, DIL_W), lambda i: (ptile(i), 0)),
                  pl.BlockSpec((ns, D), lambda i: (0, 0)),
                  pl.BlockSpec((ns, DIL_W), lambda i: (0, 0)),
                  *gate_specs, gp, gs, any_spec, any_spec, any_spec],
        out_specs=pl.BlockSpec((tm, D), lambda i: (i, 0)),
        out_shape=jax.ShapeDtypeStruct((nt, D), F32),
        scratch_shapes=[pltpu.VMEM((D, D), BF16), pltpu.VMEM((DIL_W, D), BF16), pltpu.VMEM((D, D), BF16),
                        pltpu.VMEM((2, W_CHUNK, D), F32), pltpu.SemaphoreType.DMA((2,))],
        compiler_params=_cp(1), name="mixer_out",
    )(x_main, x_samp, ya_p, yb_p, ya_s, yb_s, *([proj] * (2 * (D // GATE_TN))), *mods.args,
      w_out_ret, w_out_dil, w_out)


def _pack_halves(h):
    a = lax.bitcast_convert_type(h[:, :D // 2].astype(BF16).astype(F32), U32)
    b = lax.bitcast_convert_type(h[:, D // 2:].astype(BF16).astype(F32), U32)
    return a | (b >> 16)


def _unpack_halves(w):
    a = lax.bitcast_convert_type(w & jnp.uint32(0xFFFF0000), F32)
    b = lax.bitcast_convert_type(w << 16, F32)
    return jnp.concatenate([a.astype(BF16), b.astype(BF16)], axis=1)


def _route_rows(x, g, sh, sc, rw, rb, cnt):
    tr = x.shape[0]
    h = _normed(x, g) * (1.0 + sc) + sh
    logits = jnp.dot(h, rw, precision=lax.Precision.HIGHEST, preferred_element_type=F32) + rb
    lane = lax.broadcasted_iota(I32, logits.shape, 1)
    work = logits
    tops = []
    member = jnp.zeros(logits.shape, F32)
    for _ in range(TOPK):
        m = jnp.max(work, axis=-1, keepdims=True)
        sel = jnp.min(jnp.where(work == m, lane, logits.shape[1]), axis=-1, keepdims=True)
        hit = lane == sel
        work = jnp.where(hit, -jnp.inf, work)
        member = jnp.where(hit, 1.0, member)
        tops.append((m, sel, hit))
    ri = lax.broadcasted_iota(I32, (tr, tr), 0)
    ci = lax.broadcasted_iota(I32, (tr, tr), 1)
    tri = jnp.where(ci < ri, 1.0, 0.0).astype(BF16)
    before = jnp.dot(tri, member.astype(BF16), preferred_element_type=F32) + cnt
    es = [jnp.exp(m - tops[0][0]) for m, _, _ in tops]
    den = es[0]
    for e in es[1:]:
        den = den + e
    choices = []
    for r, (m, sel, hit) in enumerate(tops):
        rank = jnp.sum(jnp.where(hit, before, 0.0), axis=-1, keepdims=True).astype(I32)
        choices.append((sel, es[r] / den, rank))
    return _pack_halves(h), choices, jnp.sum(member, axis=0, keepdims=True)


def _route_body(x_ref, g_ref, shp_ref, scp_ref, shs_ref, scs_ref, rw_ref, rb_ref,
                hp_ref, idx_ref, wt_ref, rank_ref, cnt_ref, *, ns):
    @pl.when(pl.program_id(0) == 0)
    def _():
        cnt_ref[...] = jnp.zeros_like(cnt_ref)

    def emit(rows, x, sh, sc):
        hp, choices, tile_cnt = _route_rows(x, g_ref[...], sh, sc, rw_ref[...], rb_ref[...], cnt_ref[...])
        cnt_ref[...] = cnt_ref[...] + tile_cnt
        hp_ref[:rows] = hp
        for r, (sel, wt, rank) in enumerate(choices):
            idx_ref[:rows, r:r + 1] = sel
            wt_ref[:rows, r:r + 1] = wt
            rank_ref[:rows, r:r + 1] = rank

    def prompt():
        emit(x_ref.shape[0], x_ref[...], shp_ref[...], scp_ref[...])

    def sample():
        emit(ns, x_ref[:ns], shs_ref[...], scs_ref[...])
        hp_ref[ns:] = jnp.zeros_like(hp_ref[ns:])
        idx_ref[ns:] = jnp.zeros_like(idx_ref[ns:])
        wt_ref[ns:] = jnp.zeros_like(wt_ref[ns:])
        rank_ref[ns:] = jnp.zeros_like(rank_ref[ns:])

    _prompt_or_sample(pl.program_id(0), pl.num_programs(0), prompt, sample)


def _route(x_all, g, mods, layer, router_w, router_b, np_rows, ns):
    nt = x_all.shape[0]
    tr = 512
    lanes = 128
    rw = jnp.zeros((D, lanes), F32).at[:, :N_EXP].set(router_w[layer])
    rb = jnp.full((1, lanes), -jnp.inf, F32).at[0, :N_EXP].set(router_b[layer])
    shp, shs = mods.specs(layer, 0, tr)
    scp, scs = mods.specs(layer, 1, tr)
    full = lambda shp_: pl.BlockSpec(shp_, lambda i: (0,) * len(shp_))
    rows = lambda w: pl.BlockSpec((tr, w), lambda i: (i, 0))
    return pl.pallas_call(
        functools.partial(_route_body, ns=ns),
        grid=(np_rows // tr + 1,),
        in_specs=[rows(D), pl.BlockSpec((None, 1, D), lambda i: (layer, 0, 0)),
                  shp, scp, shs, scs, full((D, lanes)), full((1, lanes))],
        out_specs=[rows(D // 2), rows(TOPK), rows(TOPK), rows(TOPK), full((1, lanes))],
        out_shape=[jax.ShapeDtypeStruct((nt, D // 2), U32), jax.ShapeDtypeStruct((nt, TOPK), I32),
                   jax.ShapeDtypeStruct((nt, TOPK), F32), jax.ShapeDtypeStruct((nt, TOPK), I32),
                   jax.ShapeDtypeStruct((1, lanes), F32)],
        compiler_params=_cp(1), name="norm2_route",
    )(x_all, g, mods.m4, mods.m4, mods.m3, mods.m3, rw, rb)


def _route_chunk(dest_hbm, dsm, sems, step, n_steps):
    def fetch(s, slot):
        return pltpu.make_async_copy(dest_hbm.at[pl.ds(s * ROUTE_CHUNK, ROUTE_CHUNK)],
                                     dsm.at[pl.ds(slot * ROUTE_CHUNK, ROUTE_CHUNK)], sems.at[slot])

    slot = step % 2

    @pl.when(step == 0)
    def _():
        fetch(0, 0).start()

    fetch(step, slot).wait()

    @pl.when(step + 1 < n_steps)
    def _():
        fetch(step + 1, 1 - slot).start()

    return slot * ROUTE_CHUNK


def _dispatch_body(last_ref, nused_ref, dest_hbm, hp_ref, xs_hbm, dsm, zeros_ref, sem_s, sem_z, sem_r,
                   *, nt, ntiles):
    i = pl.program_id(0)

    def zero_tile(t):
        return pltpu.make_async_copy(zeros_ref, xs_hbm.at[pl.ds(t * MOE_TM, MOE_TM), :], sem_z)

    @pl.when(i == 0)
    def _():
        zeros_ref[...] = jnp.zeros_like(zeros_ref)
        for e in range(N_EXP):
            @pl.when(last_ref[e] >= 0)
            def _():
                zero_tile(last_ref[e]).start()

        def fill(t, c):
            zero_tile(t).start()
            return c

        lax.fori_loop(nused_ref[0], ntiles, fill, 0)
        for e in range(N_EXP):
            @pl.when(last_ref[e] >= 0)
            def _():
                zero_tile(0).wait()

        def fill_wait(t, c):
            zero_tile(0).wait()
            return c

        lax.fori_loop(nused_ref[0], ntiles, fill_wait, 0)

    slot = _route_chunk(dest_hbm, dsm, sem_s, i, pl.num_programs(0))
    rows = jnp.minimum(TOK_CHUNK, nt - i * TOK_CHUNK)

    def row_copy(r, d):
        return pltpu.make_async_copy(hp_ref.at[pl.ds(r, 1), :], xs_hbm.at[pl.ds(d, 1), :], sem_r)

    def issue(r, c):
        for k in range(TOPK):
            row_copy(r, dsm[slot + r * TOPK + k]).start()
        return c

    lax.fori_loop(0, rows, issue, 0)

    def drain(r, c):
        for k in range(TOPK):
            row_copy(0, 0).wait()
        return c

    lax.fori_loop(0, rows, drain, 0)


def _dispatch(hp, dest_pad, last_tile, nused, ntiles):
    nt = hp.shape[0]
    steps = dest_pad.shape[0] // ROUTE_CHUNK
    return pl.pallas_call(
        functools.partial(_dispatch_body, nt=nt, ntiles=ntiles),
        grid_spec=pltpu.PrefetchScalarGridSpec(
            num_scalar_prefetch=2,
            grid=(steps,),
            in_specs=[pl.BlockSpec(memory_space=pl.ANY),
                      pl.BlockSpec((TOK_CHUNK, D // 2), lambda i, last, nu: (i, 0))],
            out_specs=pl.BlockSpec(memory_space=pl.ANY),
            scratch_shapes=[pltpu.SMEM((2 * ROUTE_CHUNK,), I32),
                            pltpu.VMEM((MOE_TM, D // 2), U32),
                            pltpu.SemaphoreType.DMA((2,)), pltpu.SemaphoreType.DMA(()),
                            pltpu.SemaphoreType.DMA(())]),
        out_shape=jax.ShapeDtypeStruct((ntiles * MOE_TM, D // 2), U32),
        compiler_params=_cp(1), name="moe_dispatch",
    )(last_tile, nused, dest_pad, hp)


def _expert_changed(texp_ref, t):
    prev = texp_ref[jnp.maximum(t - 1, 0)]
    return (t == 0) | (texp_ref[t] != prev)


def _by_valid_rows(valid, out_ref, compute_rows):
    half = MOE_TM // 2

    @pl.when(valid > half)
    def _():
        compute_rows(MOE_TM)

    @pl.when((valid > 0) & (valid <= half))
    def _():
        compute_rows(half)
        out_ref[half:] = jnp.zeros_like(out_ref[half:])

    @pl.when(valid == 0)
    def _():
        out_ref[...] = jnp.zeros_like(out_ref)


def _gate_up_body(texp_ref, nused_ref, tvalid_ref, x_ref, wg_ref, wu_ref, bg_ref, bu_ref, act_ref, wg_bf, wu_bf):
    del nused_ref
    t = pl.program_id(1)
    valid = tvalid_ref[t]

    @pl.when((valid > 0) & _expert_changed(texp_ref, t))
    def _():
        wg_bf[...] = wg_ref[...].astype(BF16)
        wu_bf[...] = wu_ref[...].astype(BF16)

    def swiglu_rows(rows):
        x = _unpack_halves(x_ref[:rows])
        gate = jnp.dot(x, wg_bf[...], preferred_element_type=F32) + bg_ref[...]
        up = jnp.dot(x, wu_bf[...], preferred_element_type=F32) + bu_ref[...]
        gate = jnp.minimum(gate, SW_LIMIT)
        up = jnp.clip(up, -SW_LIMIT, SW_LIMIT)
        act_ref[:rows] = ((up + 1.0) * gate * jax.nn.sigmoid(SW_ALPHA * gate)).astype(act_ref.dtype)

    _by_valid_rows(valid, act_ref, swiglu_rows)


def _gate_up(xs, texp, nused, tvalid, w_gate_up, b_gate_up, layer):
    p_rows = xs.shape[0]
    ntiles = p_rows // MOE_TM
    tf = 1024
    nf = DFF // tf
    tile = lambda t, nu: jnp.minimum(t, nu[0] - 1)
    bgu = b_gate_up.reshape(N_LAYERS, N_EXP, 1, 2 * DFF)
    wspec = lambda off: pl.BlockSpec((None, None, D, tf),
                                     lambda f, t, te, nu, tv: (layer, te[tile(t, nu)], 0, off + f))
    bspec = lambda off: pl.BlockSpec((None, None, 1, tf),
                                     lambda f, t, te, nu, tv: (layer, te[tile(t, nu)], 0, off + f))
    return pl.pallas_call(
        _gate_up_body,
        grid_spec=pltpu.PrefetchScalarGridSpec(
            num_scalar_prefetch=3,
            grid=(nf, ntiles),
            in_specs=[pl.BlockSpec((MOE_TM, D // 2), lambda f, t, te, nu, tv: (tile(t, nu), 0)),
                      wspec(0), wspec(nf), bspec(0), bspec(nf)],
            out_specs=pl.BlockSpec((MOE_TM, tf), lambda f, t, te, nu, tv: (t, f)),
            scratch_shapes=[pltpu.VMEM((D, tf), BF16), pltpu.VMEM((D, tf), BF16)]),
        out_shape=jax.ShapeDtypeStruct((p_rows, DFF), BF16),
        compiler_params=_cp(2), name="moe_gate_up",
    )(texp, nused, tvalid, xs, w_gate_up, w_gate_up, bgu, bgu)


def _down_body(texp_ref, nused_ref, tvalid_ref, a_ref, w_ref, b_ref, y_ref, w_bf):
    del nused_ref
    t = pl.program_id(1)
    valid = tvalid_ref[t]

    @pl.when((valid > 0) & _expert_changed(texp_ref, t))
    def _():
        w_bf[...] = w_ref[...].astype(BF16)

    def down_rows(rows):
        y_ref[:rows] = jnp.dot(a_ref[:rows], w_bf[...], preferred_element_type=F32) + b_ref[...]

    _by_valid_rows(valid, y_ref, down_rows)


def _down(act, texp, nused, tvalid, w_down, b_down, layer):
    p_rows = act.shape[0]
    ntiles = p_rows // MOE_TM
    tn = DOWN_TN
    tile = lambda t, nu: jnp.minimum(t, nu[0] - 1)
    bd = b_down.reshape(N_LAYERS, N_EXP, 1, D)
    return pl.pallas_call(
        _down_body,
        grid_spec=pltpu.PrefetchScalarGridSpec(
            num_scalar_prefetch=3,
            grid=(D // tn, ntiles),
            in_specs=[pl.BlockSpec((MOE_TM, DFF), lambda n, t, te, nu, tv: (tile(t, nu), 0)),
                      pl.BlockSpec((None, None, DFF, tn),
                                   lambda n, t, te, nu, tv: (layer, te[tile(t, nu)], 0, n)),
                      pl.BlockSpec((None, None, 1, tn),
                                   lambda n, t, te, nu, tv: (layer, te[tile(t, nu)], 0, n))],
            out_specs=pl.BlockSpec((MOE_TM, tn), lambda n, t, te, nu, tv: (t, n)),
            scratch_shapes=[pltpu.VMEM((DFF, tn), BF16)]),
        out_shape=jax.ShapeDtypeStruct((p_rows, D), F32),
        compiler_params=_cp(2), name="moe_down",
    )(texp, nused, tvalid, act, w_down, bd)


def _combine_body(dest_hbm, ys_hbm, x_ref, wt_ref, gp_ref, gs_ref, o_ref, dsm, buf, sem_s, sem_r, *, ns):
    i = pl.program_id(0)
    slot = _route_chunk(dest_hbm, dsm, sem_s, i, pl.num_programs(0))

    def row_copy(r, k, d):
        return pltpu.make_async_copy(ys_hbm.at[pl.ds(d, 1), :], buf.at[k, pl.ds(r, 1), :], sem_r)

    def combine(rows, gate_ref):
        def issue(r, c):
            for k in range(TOPK):
                row_copy(r, k, dsm[slot + r * TOPK + k]).start()
            return c

        lax.fori_loop(0, rows, issue, 0)

        def drain(r, c):
            for k in range(TOPK):
                row_copy(0, 0, 0).wait()
            return c

        lax.fori_loop(0, rows, drain, 0)
        wt = wt_ref[:rows]
        moe = buf[0, :rows] * wt[:, 0:1]
        for k in range(1, TOPK):
            moe = moe + buf[k, :rows] * wt[:, k:k + 1]
        o_ref[:rows] = x_ref[:rows] + gate_ref[...] * moe

    def prompt():
        combine(x_ref.shape[0], gp_ref)

    def sample():
        combine(ns, gs_ref)
        o_ref[ns:] = jnp.zeros_like(o_ref[ns:])

    _prompt_or_sample(i, pl.num_programs(0), prompt, sample)


def _combine(x_all, ys, dest_pad, wts, mods, layer, np_rows, ns):
    nt = x_all.shape[0]
    assert np_rows % TOK_CHUNK == 0 and ns <= TOK_CHUNK
    any_spec = pl.BlockSpec(memory_space=pl.ANY)
    gp, gs = mods.specs(layer, 2, TOK_CHUNK)
    return pl.pallas_call(
        functools.partial(_combine_body, ns=ns),
        grid=(np_rows // TOK_CHUNK + 1,),
        in_specs=[any_spec, any_spec,
                  pl.BlockSpec((TOK_CHUNK, D), lambda i: (i, 0)),
                  pl.BlockSpec((TOK_CHUNK, TOPK), lambda i: (i, 0)),
                  gp, gs],
        out_specs=pl.BlockSpec((TOK_CHUNK, D), lambda i: (i, 0)),
        out_shape=jax.ShapeDtypeStruct((nt, D), F32),
        scratch_shapes=[pltpu.SMEM((2 * ROUTE_CHUNK,), I32), pltpu.VMEM((TOPK, TOK_CHUNK, D), F32),
                        pltpu.SemaphoreType.DMA((2,)), pltpu.SemaphoreType.DMA(())],
        compiler_params=_cp(1), name="moe_combine",
    )(dest_pad, ys, x_all, wts, *mods.args)


def _moe(x_all, g, mods, layer, router_w, router_b, w_gate_up, b_gate_up, w_down, b_down, np_rows, ns):
    nt = x_all.shape[0]
    hp, idx, wts, rank, cnt = _route(x_all, g, mods, layer, router_w, router_b, np_rows, ns)
    counts = cnt[0, :N_EXP].astype(I32)
    padded = ((counts + MOE_TM - 1) // MOE_TM) * MOE_TM
    ends = jnp.cumsum(padded)
    gstart = ends - padded
    dest = gstart[idx] + rank
    n_entries = nt * TOPK
    n_chunks = -(-n_entries // ROUTE_CHUNK)
    dest_pad = jnp.zeros((n_chunks * ROUTE_CHUNK,), I32).at[:n_entries].set(dest.reshape(-1))
    ntiles = -(-n_entries // MOE_TM) + N_EXP
    tile_ends = ends // MOE_TM
    tile_ids = jnp.arange(ntiles, dtype=I32)
    texp = jnp.minimum(jnp.sum((tile_ids[:, None] >= tile_ends[None, :]).astype(I32), axis=1), N_EXP - 1)
    nused = tile_ends[-1:].astype(I32)
    last_tile = jnp.where(padded > 0, tile_ends - 1, -1).astype(I32)
    tvalid = jnp.clip((gstart + counts)[texp] - tile_ids * MOE_TM, 0, MOE_TM)
    tvalid = jnp.where(tile_ids < nused[0], tvalid, 0).astype(I32)
    xs = _dispatch(hp, dest_pad, last_tile, nused, ntiles)
    act = _gate_up(xs, texp, nused, tvalid, w_gate_up, b_gate_up, layer)
    ys = _down(act, texp, nused, tvalid, w_down, b_down, layer)
    return _combine(x_all, ys, dest_pad, wts, mods, layer, np_rows, ns)


def _final_norm_body(x_ref, g_ref, o_ref):
    o_ref[...] = _normed(x_ref[...], g_ref[...])


def _final_norm(x_all, g, np_rows, ns):
    tr = 1024
    gspec = pl.BlockSpec((1, D), lambda i: (0, 0))
    yp = pl.pallas_call(
        _final_norm_body, grid=(np_rows // tr,),
        in_specs=[pl.BlockSpec((tr, D), lambda i: (i, 0)), gspec],
        out_specs=pl.BlockSpec((tr, D), lambda i: (i, 0)),
        out_shape=jax.ShapeDtypeStruct((np_rows, D), F32),
        compiler_params=_cp(1), name="final_norm_prompt",
    )(x_all, g)
    sblk = np_rows // ns
    ysm = pl.pallas_call(
        _final_norm_body, grid=(1,),
        in_specs=[pl.BlockSpec((ns, D), lambda i: (sblk, 0)), gspec],
        out_specs=pl.BlockSpec((ns, D), lambda i: (0, 0)),
        out_shape=jax.ShapeDtypeStruct((ns, D), F32),
        compiler_params=_cp(1), name="final_norm_sample",
    )(x_all, g)
    return yp, ysm


def kernel(x_prompt, x_sample, state_ret, cache_kv_w128_d1, cache_kv_w512_d4, cache_kv_w2048_d16, c_prompt, c_sample, norm1_g, ada1_w, ada1_b, w_in, w_out_ret, w_out_dil, w_out, norm2_g, ada2_w, ada2_b, router_w, router_b, w_gate_up, b_gate_up, w_down, b_down, final_norm_g):
    nb, seq, _ = x_prompt.shape
    ns, dec_seq, _ = x_sample.shape
    assert dec_seq == 1 and seq == 2048 and ns % 16 == 0
    past = cache_kv_w2048_d16.shape[2]
    np_rows = nb * seq
    assert np_rows % ns == 0
    caches = (cache_kv_w128_d1, cache_kv_w512_d4, cache_kv_w2048_d16)

    x_src = (x_prompt.reshape(np_rows, D), x_sample.reshape(ns, D), 0)
    nc = -(-(ns + nb) // 16) * 16
    c_all = jnp.zeros((nc, D), F32).at[:ns].set(c_sample).at[ns:ns + nb].set(c_prompt)
    mods1 = _Mods(_mods(c_all, ada1_w, ada1_b), ns, nb, seq)
    mods2 = _Mods(_mods(c_all, ada2_w, ada2_b), ns, nb, seq)
    g1 = norm1_g.reshape(N_LAYERS, 1, D)
    g2 = norm2_g.reshape(N_LAYERS, 1, D)

    ret_p, ret_s, projs, kv_s = [], [], [], [[] for _ in DIL]
    for layer in range(N_LAYERS):
        h = _norm_mod(x_src, g1, mods1, layer, np_rows, ns)
        proj = _proj(h, w_in, layer)
        projs.append(proj)
        ya_p, st_p = _ret_prompt(proj, nb, seq)
        ya_s, st_s = _ret_sample(proj, state_ret, layer, np_rows, ns, past)
        yb_p = _dil_prompt(proj, nb, seq)
        yb_s = _dil_sample(proj, caches, layer, np_rows, ns)
        x_all = _mix_out(x_src, ya_p, yb_p, ya_s, yb_s, proj, w_out_ret, w_out_dil, w_out, mods1, layer)
        x_all = _moe(x_all, g2, mods2, layer, router_w, router_b, w_gate_up, b_gate_up, w_down, b_down,
                     np_rows, ns)
        x_src = (x_all, x_all, np_rows // ns)
        ret_p.append(st_p)
        ret_s.append(st_s)
        for g in range(len(DIL)):
            c0 = OFF_DIL + g * 3 * DIL_W + DIL_W
            kv_s[g].append(proj[np_rows:, c0:c0 + 2 * DIL_W].reshape(ns, 1, 2, DIL_H, DIL_E))
    y_p, y_s = _final_norm(x_all, final_norm_g.reshape(1, D), np_rows, ns)
    kv_p = [_kv_out(projs, g, nb, seq) for g in range(len(DIL))]
    return (y_p.reshape(nb, seq, D), y_s.reshape(ns, 1, D),
            jnp.stack(ret_p), kv_p[0], kv_p[1], kv_p[2],
            jnp.stack(ret_s), jnp.stack(kv_s[0]), jnp.stack(kv_s[1]), jnp.stack(kv_s[2]))
```

```python
import functools

import jax
import jax.numpy as jnp
from jax import lax
from jax.experimental import pallas as pl
from jax.experimental.pallas import tpu as pltpu

F32 = jnp.float32
BF16 = jnp.bfloat16
U32 = jnp.uint32
I32 = jnp.int32

D = 2048
N_LAYERS = 2
RET_H = 8
RET_DK = 128
RET_DV = 256
RET_C = 128
ROPE_BASE = 10000.0
DIL = ((128, 1), (512, 4), (2048, 16))
DIL_H = 4
DIL_E = 128
DIL_W = DIL_H * DIL_E
N_EXP = 32
TOPK = 4
DFF = 2048
SW_LIMIT = 7.0
SW_ALPHA = 1.702
EPS = 1e-6

OFF_Q = 0
OFF_K = 1024
OFF_V = 2048
OFF_G = 4096
OFF_DIL = 6144
OFF_GA = 10752
OFF_GB = 12800
IN_W = 14848

VMEM_LIMIT = 56 * 1024 * 1024
MOE_TM = 512
DOWN_TN = 2048
ROUTE_CHUNK = 2048
TOK_CHUNK = ROUTE_CHUNK // TOPK


def _cp(n_grid_dims):
    return pltpu.CompilerParams(
        dimension_semantics=("arbitrary",) * n_grid_dims,
        vmem_limit_bytes=VMEM_LIMIT)


def _row_tile(n, cap):
    for k in range(1, n + 1):
        if n % k == 0 and n // k <= cap and (n // k) % 16 == 0:
            return n // k
    raise ValueError((n, cap))


def _silu(x):
    return x * jax.nn.sigmoid(x)


def _mods_body(c_ref, w_ref, b_ref, o_ref):
    a = _silu(c_ref[...]).astype(BF16)
    o_ref[...] = jnp.dot(a, w_ref[...].astype(BF16), preferred_element_type=F32) + b_ref[...]


def _mods(c_all, ada_w, ada_b):
    nc = c_all.shape[0]
    tn = 1024
    return pl.pallas_call(
        _mods_body,
        grid=(N_LAYERS, 3 * D // tn),
        in_specs=[pl.BlockSpec((nc, D), lambda l, n: (0, 0)),
                  pl.BlockSpec((None, D, tn), lambda l, n: (l, 0, n)),
                  pl.BlockSpec((None, 1, tn), lambda l, n: (l, 0, n))],
        out_specs=pl.BlockSpec((None, nc, tn), lambda l, n: (l, 0, n)),
        out_shape=jax.ShapeDtypeStruct((N_LAYERS, nc, 3 * D), F32),
        compiler_params=_cp(2), name="ada_mods",
    )(c_all, ada_w, ada_b.reshape(N_LAYERS, 1, 3 * D))


class _Mods:
    def __init__(self, m, ns, nb, seq):
        self.m3 = m
        self.m4 = m.reshape(m.shape[0], m.shape[1], 1, 3 * D)
        self.ns, self.nb, self.seq = ns, nb, seq

    @property
    def args(self):
        return (self.m4, self.m3)

    def specs(self, layer, which, tr, tn=D, col_axis=False):
        per = self.seq // tr
        ns, nb = self.ns, self.nb
        nblk = D // tn
        if col_axis:
            p_map = lambda n, i: (layer, ns + jnp.minimum(i // per, nb - 1), 0, which * nblk + n)
            s_map = lambda n, i: (layer, 0, which * nblk + n)
        else:
            p_map = lambda i: (layer, ns + jnp.minimum(i // per, nb - 1), 0, which * nblk)
            s_map = lambda i: (layer, 0, which * nblk)
        return [pl.BlockSpec((None, None, 1, tn), p_map), pl.BlockSpec((None, ns, tn), s_map)]


def _prompt_or_sample(step, n_steps, prompt_fn, sample_fn):
    pl.when(step < n_steps - 1)(prompt_fn)
    pl.when(step == n_steps - 1)(sample_fn)


def _normed(x, g):
    return x * lax.rsqrt(jnp.mean(x * x, axis=-1, keepdims=True) + EPS) * g


def _norm_mod_body(xp_ref, xs_ref, g_ref, shp_ref, scp_ref, shs_ref, scs_ref, o_ref, *, ns):
    def prompt():
        y = _normed(xp_ref[...], g_ref[...])
        o_ref[...] = (y * (1.0 + scp_ref[...]) + shp_ref[...]).astype(o_ref.dtype)

    def sample():
        y = _normed(xs_ref[...], g_ref[...])
        o_ref[:ns] = (y * (1.0 + scs_ref[...]) + shs_ref[...]).astype(o_ref.dtype)
        o_ref[ns:] = jnp.zeros_like(o_ref[ns:])

    _prompt_or_sample(pl.program_id(0), pl.num_programs(0), prompt, sample)


def _norm_mod(x_src, g, mods, layer, np_rows, ns):
    x_main, x_samp, sblk = x_src
    tr = 1024
    npt = np_rows // tr
    shp, shs = mods.specs(layer, 0, tr)
    scp, scs = mods.specs(layer, 1, tr)
    return pl.pallas_call(
        functools.partial(_norm_mod_body, ns=ns),
        grid=(npt + 1,),
        in_specs=[pl.BlockSpec((tr, D), lambda i: (jnp.minimum(i, npt - 1), 0)),
                  pl.BlockSpec((ns, D), lambda i: (sblk, 0)),
                  pl.BlockSpec((None, 1, D), lambda i: (layer, 0, 0)),
                  shp, scp, shs, scs],
        out_specs=pl.BlockSpec((tr, D), lambda i: (i, 0)),
        out_shape=jax.ShapeDtypeStruct((np_rows + ns, D), BF16),
        compiler_params=_cp(1), name="norm1",
    )(x_main, x_samp, g, mods.m4, mods.m4, mods.m3, mods.m3)


def _proj_body(x_ref, w_ref, o_ref, wbf_ref):
    @pl.when(pl.program_id(1) == 0)
    def _():
        wbf_ref[...] = w_ref[...].astype(BF16)

    o_ref[...] = jnp.dot(x_ref[...], wbf_ref[...], preferred_element_type=F32).astype(o_ref.dtype)


def _proj(h, w_in, layer):
    nt = h.shape[0]
    tm = _row_tile(nt, 1376)
    tn = 1024
    return pl.pallas_call(
        _proj_body,
        grid=(pl.cdiv(IN_W, tn), nt // tm),
        in_specs=[pl.BlockSpec((tm, D), lambda n, m: (m, 0)),
                  pl.BlockSpec((None, D, tn), lambda n, m: (layer, 0, n))],
        out_specs=pl.BlockSpec((tm, tn), lambda n, m: (m, n)),
        out_shape=jax.ShapeDtypeStruct((nt, IN_W), F32),
        scratch_shapes=[pltpu.VMEM((D, tn), BF16)],
        compiler_params=_cp(2), name="in_proj",
    )(h, w_in)


def _rope_tables(pos):
    half = RET_DK // 2
    inv = ROPE_BASE ** -jnp.linspace(0.0, 1.0, half, dtype=F32)
    ang = pos.astype(F32)[:, None] * inv[None, :]
    cos = jnp.cos(ang)
    sin = jnp.sin(ang)
    return jnp.concatenate([cos, cos], axis=-1), jnp.concatenate([-sin, sin], axis=-1)


def _decay_tables(chunk):
    log_g = jnp.log1p(-jnp.exp2(-5.0 - jnp.arange(RET_H, dtype=F32)))
    idx = jnp.arange(chunk, dtype=F32)
    rel = idx[:, None] - idx[None, :]
    inner = jnp.where(rel >= 0, jnp.exp(log_g[:, None, None] * jnp.maximum(rel, 0.0)), 0.0)
    q_decay = jnp.exp(log_g[:, None] * (idx + 1.0)[None, :])
    k_decay = jnp.exp(log_g[:, None] * (chunk - 1.0 - idx)[None, :])
    chunk_decay = jnp.exp(log_g * chunk)
    return inner, q_decay, k_decay, chunk_decay


def _rot(x, cos, sin):
    return x * cos + pltpu.roll(x, RET_DK // 2, 1) * sin


def _head_norm_gate(o, g):
    on = o * lax.rsqrt(jnp.mean(o * o, axis=-1, keepdims=True) + EPS)
    return _silu(g) * on


def _ret_prompt_body(q_ref, k_ref, v_ref, g_ref, cos_ref, sin_ref, inner_ref, qd_ref, kd_ref, cd_ref,
                     ya_ref, s_ref):
    @pl.when(pl.program_id(1) == 0)
    def _():
        s_ref[...] = jnp.zeros_like(s_ref)

    cos = cos_ref[...]
    sin = sin_ref[...]
    for h in range(RET_H):
        q = _rot(q_ref[:, h * RET_DK:(h + 1) * RET_DK], cos, sin)
        k = _rot(k_ref[:, h * RET_DK:(h + 1) * RET_DK], cos, sin) * (RET_DK ** -0.5)
        vb = v_ref[:, h * RET_DV:(h + 1) * RET_DV].astype(BF16)
        s = lax.dot_general(q.astype(BF16), k.astype(BF16), (((1,), (1,)), ((), ())),
                            preferred_element_type=F32) * inner_ref[h]
        state = s_ref[h]
        o = (jnp.dot(s.astype(BF16), vb, preferred_element_type=F32)
             + jnp.dot((q * qd_ref[h]).astype(BF16), state.astype(BF16), preferred_element_type=F32))
        kt = (k * kd_ref[h]).T.astype(BF16)
        s_ref[h] = state * cd_ref[h] + jnp.dot(kt, vb, preferred_element_type=F32)
        y = _head_norm_gate(o, g_ref[:, h * RET_DV:(h + 1) * RET_DV])
        ya_ref[:, h * RET_DV:(h + 1) * RET_DV] = y.astype(ya_ref.dtype)


def _ret_prompt(proj, nb, seq):
    nch = seq // RET_C
    cos, sin = _rope_tables(jnp.arange(seq, dtype=I32))
    inner, qd, kd, cd = _decay_tables(RET_C)
    qd_b = jnp.broadcast_to(qd[:, :, None], (RET_H, RET_C, RET_DK))
    kd_b = jnp.broadcast_to(kd[:, :, None], (RET_H, RET_C, RET_DK))
    cd_b = jnp.broadcast_to(cd[:, None, None], (RET_H, 1, RET_DV))
    row = lambda b, c: b * nch + c
    full3 = lambda shp: pl.BlockSpec(shp, lambda b, c: (0, 0, 0))
    return pl.pallas_call(
        _ret_prompt_body,
        grid=(nb, nch),
        in_specs=[pl.BlockSpec((RET_C, 1024), lambda b, c: (row(b, c), OFF_Q // 1024)),
                  pl.BlockSpec((RET_C, 1024), lambda b, c: (row(b, c), OFF_K // 1024)),
                  pl.BlockSpec((RET_C, 2048), lambda b, c: (row(b, c), OFF_V // 2048)),
                  pl.BlockSpec((RET_C, 2048), lambda b, c: (row(b, c), OFF_G // 2048)),
                  pl.BlockSpec((RET_C, RET_DK), lambda b, c: (c, 0)),
                  pl.BlockSpec((RET_C, RET_DK), lambda b, c: (c, 0)),
                  full3((RET_H, RET_C, RET_C)), full3((RET_H, RET_C, RET_DK)),
                  full3((RET_H, RET_C, RET_DK)), full3((RET_H, 1, RET_DV))],
        out_specs=[pl.BlockSpec((RET_C, D), lambda b, c: (row(b, c), 0)),
                   pl.BlockSpec((None, RET_H, RET_DK, RET_DV), lambda b, c: (b, 0, 0, 0))],
        out_shape=[jax.ShapeDtypeStruct((nb * seq, D), BF16),
                   jax.ShapeDtypeStruct((nb, RET_H, RET_DK, RET_DV), F32)],
        compiler_params=_cp(2), name="retention_prompt",
    )(proj, proj, proj, proj, cos, sin, inner, qd_b, kd_b, cd_b)


RS_NB = 8


def _ret_sample_body(q_ref, k_ref, v_ref, g_ref, cos_ref, sin_ref, inner_ref, qd_ref, kd_ref, cd_ref,
                     st_ref, ya_ref, so_ref):
    i = pl.program_id(0)
    cos = cos_ref[...]
    sin = sin_ref[...]
    row_i = lax.broadcasted_iota(I32, (RET_DK, RET_DK), 0)
    col_i = lax.broadcasted_iota(I32, (RET_DK, RET_DK), 1)
    eye = row_i == col_i
    rows = pl.ds(pl.multiple_of(i * RS_NB, RS_NB), RS_NB)
    q_all = q_ref[rows, :]
    k_all = k_ref[rows, :]
    v_all = v_ref[rows, :]
    g_all = g_ref[rows, :]
    out_rows = []
    for j in range(RS_NB):
        heads = []
        for h in range(RET_H):
            q = _rot(q_all[j:j + 1, h * RET_DK:(h + 1) * RET_DK], cos, sin)
            k = _rot(k_all[j:j + 1, h * RET_DK:(h + 1) * RET_DK], cos, sin) * (RET_DK ** -0.5)
            v = v_all[j:j + 1, h * RET_DV:(h + 1) * RET_DV]
            s = jnp.sum(q * k, axis=-1, keepdims=True) * inner_ref[h]
            state = st_ref[j, h]
            q8 = jnp.broadcast_to(q * qd_ref[h], (8, RET_DK)).astype(BF16)
            qs = jnp.dot(q8, state.astype(BF16), preferred_element_type=F32)[0:1]
            o = s * v + qs
            kdiag = jnp.where(eye, jnp.broadcast_to(k * kd_ref[h], (RET_DK, RET_DK)), 0.0).astype(BF16)
            vb = jnp.broadcast_to(v, (RET_DK, RET_DV)).astype(BF16)
            so_ref[j, h] = state * cd_ref[h] + jnp.dot(kdiag, vb, preferred_element_type=F32)
            heads.append(_head_norm_gate(o, g_all[j:j + 1, h * RET_DV:(h + 1) * RET_DV]))
        out_rows.append(jnp.concatenate(heads, axis=1))
    ya_ref[...] = jnp.concatenate(out_rows, axis=0)


def _ret_sample(proj, state_ret, layer, np_rows, ns, past):
    cos, sin = _rope_tables(past + jnp.arange(1, dtype=I32))
    inner, qd, kd, cd = _decay_tables(1)
    inner_b = inner.reshape(RET_H, 1, 1)
    qd_b = jnp.broadcast_to(qd[:, :, None], (RET_H, 1, RET_DK))
    kd_b = jnp.broadcast_to(kd[:, :, None], (RET_H, 1, RET_DK))
    cd_b = jnp.broadcast_to(cd[:, None, None], (RET_H, 1, RET_DV))
    sblk = np_rows // ns
    full = lambda shp: pl.BlockSpec(shp, lambda i: (0,) * len(shp))
    return pl.pallas_call(
        _ret_sample_body,
        grid=(ns // RS_NB,),
        in_specs=[pl.BlockSpec((ns, 1024), lambda i: (sblk, OFF_Q // 1024)),
                  pl.BlockSpec((ns, 1024), lambda i: (sblk, OFF_K // 1024)),
                  pl.BlockSpec((ns, 2048), lambda i: (sblk, OFF_V // 2048)),
                  pl.BlockSpec((ns, 2048), lambda i: (sblk, OFF_G // 2048)),
                  full((1, RET_DK)), full((1, RET_DK)),
                  full((RET_H, 1, 1)), full((RET_H, 1, RET_DK)), full((RET_H, 1, RET_DK)),
                  full((RET_H, 1, RET_DV)),
                  pl.BlockSpec((None, RS_NB, RET_H, RET_DK, RET_DV), lambda i: (layer, i, 0, 0, 0))],
        out_specs=[pl.BlockSpec((RS_NB, D), lambda i: (i, 0)),
                   pl.BlockSpec((RS_NB, RET_H, RET_DK, RET_DV), lambda i: (i, 0, 0, 0))],
        out_shape=[jax.ShapeDtypeStruct((ns, D), F32),
                   jax.ShapeDtypeStruct((ns, RET_H, RET_DK, RET_DV), F32)],
        compiler_params=_cp(1), name="retention_sample",
    )(proj, proj, proj, proj, cos, sin, inner_b, qd_b, kd_b, cd_b, state_ret)


def _merge_groups(parts):
    m_all = parts[0][1]
    for _, m_g, _ in parts[1:]:
        m_all = jnp.maximum(m_all, m_g)
    num = 0.0
    den = 0.0
    for n_g, m_g, l_g in parts:
        w = jnp.exp(m_g - m_all)
        num = num + n_g * w
        den = den + l_g * w
    return num / den


def _dil_prompt_body(*refs, seq):
    qkv = refs[:9]
    yb_ref = refs[9]
    num_ref, m_ref, l_ref = refs[10:13]
    band = 128
    scale = DIL_E ** -0.5
    qi = lax.broadcasted_iota(I32, (band, band), 0)
    kj1 = lax.broadcasted_iota(I32, (band, band), 1)
    qi2 = lax.broadcasted_iota(I32, (band, 2 * band), 0)
    kj2 = lax.broadcasted_iota(I32, (band, 2 * band), 1)
    first_valid = kj1 <= qi
    later_valid = (kj2 >= qi2) & (kj2 <= qi2 + band)
    for g, (window, dil) in enumerate(DIL):
        assert window // dil == band
        q_ref, k_ref, v_ref = qkv[3 * g:3 * g + 3]
        nblk = seq // dil // band
        for p in range(dil):
            for n in range(nblk):
                rows_q = pl.ds(p + n * band * dil, band, stride=dil)
                qb = q_ref[rows_q, :].astype(BF16)
                if n == 0:
                    rows_k = rows_q
                    valid = first_valid
                else:
                    rows_k = pl.ds(p + (n - 1) * band * dil, 2 * band, stride=dil)
                    valid = later_valid
                kk = k_ref[rows_k, :].astype(BF16)
                vv = v_ref[rows_k, :].astype(BF16)
                s = lax.dot_general(qb, kk, (((1,), (1,)), ((), ())), preferred_element_type=F32) * scale
                s = jnp.where(valid, s, -jnp.inf)
                m = jnp.max(s, axis=-1, keepdims=True)
                pe = jnp.exp(s - m)
                l = jnp.sum(pe, axis=-1, keepdims=True)
                num_ref[g, rows_q, :] = jnp.dot(pe.astype(BF16), vv, preferred_element_type=F32)
                m_ref[g, rows_q, :] = jnp.broadcast_to(m, (band, DIL_E))
                l_ref[g, rows_q, :] = jnp.broadcast_to(l, (band, DIL_E))
    parts = [(num_ref[g], m_ref[g], l_ref[g]) for g in range(len(DIL))]
    yb_ref[...] = _merge_groups(parts).astype(yb_ref.dtype)


def _dil_prompt(proj, nb, seq):
    in_specs = []
    for g in range(len(DIL)):
        for j in range(3):
            cb = (OFF_DIL + g * 3 * DIL_W + j * DIL_W) // DIL_E
            in_specs.append(pl.BlockSpec((seq, DIL_E), lambda b, h, cb=cb: (b, cb + h)))
    scr = pltpu.VMEM((len(DIL), seq, DIL_E), F32)
    return pl.pallas_call(
        functools.partial(_dil_prompt_body, seq=seq),
        grid=(nb, DIL_H),
        in_specs=in_specs,
        out_specs=pl.BlockSpec((seq, DIL_E), lambda b, h: (b, h)),
        out_shape=jax.ShapeDtypeStruct((nb * seq, DIL_W), BF16),
        scratch_shapes=[scr, scr, scr],
        compiler_params=_cp(2), name="dilated_prompt",
    )(*([proj] * 9))


DS_NB = 8


def _dil_sample_body(c0_ref, c1_ref, c2_ref, q0_ref, q1_ref, q2_ref, yb_ref):
    i = pl.program_id(0)
    caches = (c0_ref, c1_ref, c2_ref)
    scale = DIL_E ** -0.5
    rows = pl.ds(pl.multiple_of(i * DS_NB, DS_NB), DS_NB)
    news = [ref[rows, :] for ref in (q0_ref, q1_ref, q2_ref)]
    out_rows = []
    for j in range(DS_NB):
        parts = []
        for g in range(len(DIL)):
            new = news[g][j:j + 1]
            q4 = jnp.concatenate([new[:, h * DIL_E:(h + 1) * DIL_E] for h in range(DIL_H)], axis=0)
            k4 = jnp.concatenate([new[:, DIL_W + h * DIL_E:DIL_W + (h + 1) * DIL_E] for h in range(DIL_H)], axis=0)
            v4 = jnp.concatenate([new[:, 2 * DIL_W + h * DIL_E:2 * DIL_W + (h + 1) * DIL_E] for h in range(DIL_H)],
                                 axis=0)
            kc = caches[g][j, :, 0:DIL_H, :]
            vc = caches[g][j, :, DIL_H:2 * DIL_H, :]
            s_c = jnp.sum(kc * q4[None], axis=-1, keepdims=True) * scale
            s_n = jnp.sum(k4 * q4, axis=-1, keepdims=True) * scale
            m = jnp.maximum(jnp.max(s_c, axis=0), s_n)
            p_c = jnp.exp(s_c - m[None])
            p_n = jnp.exp(s_n - m)
            l = jnp.sum(p_c, axis=0) + p_n
            num = jnp.sum(p_c * vc, axis=0) + p_n * v4
            parts.append((num, m, l))
        y = _merge_groups(parts)
        out_rows.append(jnp.concatenate([y[h:h + 1] for h in range(DIL_H)], axis=1))
    yb_ref[...] = jnp.concatenate(out_rows, axis=0)


def _dil_sample(proj, caches, layer, np_rows, ns):
    sblk = np_rows // ns
    views = []
    in_specs = []
    for cache, (window, dil) in zip(caches, DIL):
        wb = cache.shape[2]
        assert wb == window and wb % dil == 0
        views.append(cache.reshape(cache.shape[0] * ns, wb // dil, dil, 2 * DIL_H, DIL_E))
        in_specs.append(pl.BlockSpec((DS_NB, wb // dil, None, 2 * DIL_H, DIL_E),
                                     lambda i: (layer * (ns // DS_NB) + i, 0, 0, 0, 0)))
    for g in range(len(DIL)):
        in_specs.append(pl.BlockSpec((ns, 3 * DIL_W), lambda i, g=g: (sblk, OFF_DIL // (3 * DIL_W) + g)))
    return pl.pallas_call(
        _dil_sample_body,
        grid=(ns // DS_NB,),
        in_specs=in_specs,
        out_specs=pl.BlockSpec((DS_NB, DIL_W), lambda i: (i, 0)),
        out_shape=jax.ShapeDtypeStruct((ns, DIL_W), F32),
        compiler_params=_cp(1), name="dilated_sample",
    )(*views, proj, proj, proj)


def _kv_out_body(*refs):
    o_ref = refs[-1]
    layer = pl.program_id(0)
    for l in range(N_LAYERS):
        @pl.when(layer == l)
        def _(k_ref=refs[2 * l], v_ref=refs[2 * l + 1]):
            for h in range(DIL_H):
                o_ref[:, 0, h, :] = k_ref[:, h * DIL_E:(h + 1) * DIL_E]
                o_ref[:, 1, h, :] = v_ref[:, h * DIL_E:(h + 1) * DIL_E]


def _kv_out(projs, g, nb, seq):
    window = min(DIL[g][0], seq)
    rows = min(window, 512)
    nr = window // rows
    cb = (OFF_DIL + g * 3 * DIL_W + DIL_W) // DIL_W
    first = (seq - window) // rows
    per_b = seq // rows
    last = (nb - 1) * per_b + first + nr - 1
    in_specs = []
    for l in range(N_LAYERS):
        def rblk(ll, b, r, l=l):
            cur = b * per_b + first + r
            return jnp.where(ll == l, cur, jnp.where(ll < l, first, last))
        in_specs.append(pl.BlockSpec((rows, DIL_W), lambda ll, b, r, f=rblk: (f(ll, b, r), cb)))
        in_specs.append(pl.BlockSpec((rows, DIL_W), lambda ll, b, r, f=rblk: (f(ll, b, r), cb + 1)))
    args = []
    for p in projs:
        args += [p, p]
    return pl.pallas_call(
        _kv_out_body,
        grid=(N_LAYERS, nb, nr),
        in_specs=in_specs,
        out_specs=pl.BlockSpec((None, None, rows, 2, DIL_H, DIL_E), lambda ll, b, r: (ll, b, r, 0, 0, 0)),
        out_shape=jax.ShapeDtypeStruct((N_LAYERS, nb, window, 2, DIL_H, DIL_E), F32),
        compiler_params=_cp(3), name="kv_prompt_out",
    )(*args)


W_CHUNK = 256


def _load_weight_bf16(w_hbm, layer, dst_bf, stage, sems):
    n = dst_bf.shape[0] // W_CHUNK

    def chunk(c):
        return pltpu.make_async_copy(w_hbm.at[layer, pl.ds(c * W_CHUNK, W_CHUNK), :], stage.at[c % 2],
                                     sems.at[c % 2])

    chunk(0).start()
    for c in range(n):
        if c + 1 < n:
            chunk(c + 1).start()
        chunk(c).wait()
        dst_bf[c * W_CHUNK:(c + 1) * W_CHUNK, :] = stage[c % 2].astype(BF16)


GATE_TN = 512


def _mix_out_body(xp_ref, xs_ref, yap_ref, ybp_ref, yas_ref, ybs_ref, *rest, ns, layer):
    n_gate = D // GATE_TN
    ga_refs, gb_refs = rest[:n_gate], rest[n_gate:2 * n_gate]
    gp_ref, gs_ref, wr_hbm, wd_hbm, wo_hbm, o_ref, wr_bf, wd_bf, wo_bf, stage, sems = rest[2 * n_gate:]

    @pl.when(pl.program_id(0) == 0)
    def _():
        _load_weight_bf16(wr_hbm, layer, wr_bf, stage, sems)
        _load_weight_bf16(wd_hbm, layer, wd_bf, stage, sems)
        _load_weight_bf16(wo_hbm, layer, wo_bf, stage, sems)

    def mixed(rows, ya, yb, x, gate):
        parts = []
        for c in range(n_gate):
            cols = slice(c * GATE_TN, (c + 1) * GATE_TN)
            y_a = jnp.dot(ya, wr_bf[:, cols], preferred_element_type=F32)
            y_b = jnp.dot(yb, wd_bf[:, cols], preferred_element_type=F32)
            merged = jax.nn.sigmoid(ga_refs[c][:rows]) * y_a + jax.nn.sigmoid(gb_refs[c][:rows]) * y_b
            parts.append(merged.astype(BF16))
        mix = jnp.dot(jnp.concatenate(parts, axis=1), wo_bf[...], preferred_element_type=F32)
        return x + gate * mix

    def prompt():
        o_ref[...] = mixed(o_ref.shape[0], yap_ref[...], ybp_ref[...], xp_ref[...], gp_ref[...])

    def sample():
        o_ref[:ns] = mixed(ns, yas_ref[...].astype(BF16), ybs_ref[...].astype(BF16), xs_ref[...], gs_ref[...])
        o_ref[ns:] = jnp.zeros_like(o_ref[ns:])

    _prompt_or_sample(pl.program_id(0), pl.num_programs(0), prompt, sample)


def _mix_out(x_src, ya_p, yb_p, ya_s, yb_s, proj, w_out_ret, w_out_dil, w_out, mods, layer):
    x_main, x_samp, sblk = x_src
    nt = proj.shape[0]
    np_rows = ya_p.shape[0]
    ns = ya_s.shape[0]
    tm = 256
    npt = np_rows // tm
    ptile = lambda i: jnp.minimum(i, npt - 1)
    gp, gs = mods.specs(layer, 2, tm)
    any_spec = pl.BlockSpec(memory_space=pl.ANY)
    gate_specs = [pl.BlockSpec((tm, GATE_TN), lambda i, cb=(off // GATE_TN + c): (i, cb))
                  for off in (OFF_GA, OFF_GB) for c in range(D // GATE_TN)]
    return pl.pallas_call(
        functools.partial(_mix_out_body, ns=ns, layer=layer),
        grid=(npt + 1,),
        in_specs=[pl.BlockSpec((tm, D), lambda i: (ptile(i), 0)),
                  pl.BlockSpec((ns, D), lambda i: (sblk, 0)),
                  pl.BlockSpec((tm, D), lambda i: (ptile(i), 0)),
                  pl.BlockSpec((tm, DIL_W), lambda i: (ptile(i), 0)),
                  pl.BlockSpec((ns, D), lambda i: (0, 0)),
                  pl.BlockSpec((ns, DIL_W), lambda i: (0, 0)),
                  *gate_specs, gp, gs, any_spec, any_spec, any_spec],
        out_specs=pl.BlockSpec((tm, D), lambda i: (i, 0)),
        out_shape=jax.ShapeDtypeStruct((nt, D), F32),
        scratch_shapes=[pltpu.VMEM((D, D), BF16), pltpu.VMEM((DIL_W, D), BF16), pltpu.VMEM((D, D), BF16),
                        pltpu.VMEM((2, W_CHUNK, D), F32), pltpu.SemaphoreType.DMA((2,))],
        compiler_params=_cp(1), name="mixer_out",
    )(x_main, x_samp, ya_p, yb_p, ya_s, yb_s, *([proj] * (2 * (D // GATE_TN))), *mods.args,
      w_out_ret, w_out_dil, w_out)


def _pack_halves(h):
    a = lax.bitcast_convert_type(h[:, :D // 2].astype(BF16).astype(F32), U32)
    b = lax.bitcast_convert_type(h[:, D // 2:].astype(BF16).astype(F32), U32)
    return a | (b >> 16)


def _unpack_halves(w):
    a = lax.bitcast_convert_type(w & jnp.uint32(0xFFFF0000), F32)
    b = lax.bitcast_convert_type(w << 16, F32)
    return jnp.concatenate([a.astype(BF16), b.astype(BF16)], axis=1)


def _route_rows(x, g, sh, sc, rw, rb, cnt):
    tr = x.shape[0]
    h = _normed(x, g) * (1.0 + sc) + sh
    logits = jnp.dot(h, rw, precision=lax.Precision.HIGHEST, preferred_element_type=F32) + rb
    lane = lax.broadcasted_iota(I32, logits.shape, 1)
    work = logits
    tops = []
    member = jnp.zeros(logits.shape, F32)
    for _ in range(TOPK):
        m = jnp.max(work, axis=-1, keepdims=True)
        sel = jnp.min(jnp.where(work == m, lane, logits.shape[1]), axis=-1, keepdims=True)
        hit = lane == sel
        work = jnp.where(hit, -jnp.inf, work)
        member = jnp.where(hit, 1.0, member)
        tops.append((m, sel, hit))
    ri = lax.broadcasted_iota(I32, (tr, tr), 0)
    ci = lax.broadcasted_iota(I32, (tr, tr), 1)
    tri = jnp.where(ci < ri, 1.0, 0.0).astype(BF16)
    before = jnp.dot(tri, member.astype(BF16), preferred_element_type=F32) + cnt
    es = [jnp.exp(m - tops[0][0]) for m, _, _ in tops]
    den = es[0]
    for e in es[1:]:
        den = den + e
    choices = []
    for r, (m, sel, hit) in enumerate(tops):
        rank = jnp.sum(jnp.where(hit, before, 0.0), axis=-1, keepdims=True).astype(I32)
        choices.append((sel, es[r] / den, rank))
    return _pack_halves(h), choices, jnp.sum(member, axis=0, keepdims=True)


def _route_body(x_ref, g_ref, shp_ref, scp_ref, shs_ref, scs_ref, rw_ref, rb_ref,
                hp_ref, idx_ref, wt_ref, rank_ref, cnt_ref, *, ns):
    @pl.when(pl.program_id(0) == 0)
    def _():
        cnt_ref[...] = jnp.zeros_like(cnt_ref)

    def emit(rows, x, sh, sc):
        hp, choices, tile_cnt = _route_rows(x, g_ref[...], sh, sc, rw_ref[...], rb_ref[...], cnt_ref[...])
        cnt_ref[...] = cnt_ref[...] + tile_cnt
        hp_ref[:rows] = hp
        for r, (sel, wt, rank) in enumerate(choices):
            idx_ref[:rows, r:r + 1] = sel
            wt_ref[:rows, r:r + 1] = wt
            rank_ref[:rows, r:r + 1] = rank

    def prompt():
        emit(x_ref.shape[0], x_ref[...], shp_ref[...], scp_ref[...])

    def sample():
        emit(ns, x_ref[:ns], shs_ref[...], scs_ref[...])
        hp_ref[ns:] = jnp.zeros_like(hp_ref[ns:])
        idx_ref[ns:] = jnp.zeros_like(idx_ref[ns:])
        wt_ref[ns:] = jnp.zeros_like(wt_ref[ns:])
        rank_ref[ns:] = jnp.zeros_like(rank_ref[ns:])

    _prompt_or_sample(pl.program_id(0), pl.num_programs(0), prompt, sample)


def _route(x_all, g, mods, layer, router_w, router_b, np_rows, ns):
    nt = x_all.shape[0]
    tr = 512
    lanes = 128
    rw = jnp.zeros((D, lanes), F32).at[:, :N_EXP].set(router_w[layer])
    rb = jnp.full((1, lanes), -jnp.inf, F32).at[0, :N_EXP].set(router_b[layer])
    shp, shs = mods.specs(layer, 0, tr)
    scp, scs = mods.specs(layer, 1, tr)
    full = lambda shp_: pl.BlockSpec(shp_, lambda i: (0,) * len(shp_))
    rows = lambda w: pl.BlockSpec((tr, w), lambda i: (i, 0))
    return pl.pallas_call(
        functools.partial(_route_body, ns=ns),
        grid=(np_rows // tr + 1,),
        in_specs=[rows(D), pl.BlockSpec((None, 1, D), lambda i: (layer, 0, 0)),
                  shp, scp, shs, scs, full((D, lanes)), full((1, lanes))],
        out_specs=[rows(D // 2), rows(TOPK), rows(TOPK), rows(TOPK), full((1, lanes))],
        out_shape=[jax.ShapeDtypeStruct((nt, D // 2), U32), jax.ShapeDtypeStruct((nt, TOPK), I32),
                   jax.ShapeDtypeStruct((nt, TOPK), F32), jax.ShapeDtypeStruct((nt, TOPK), I32),
                   jax.ShapeDtypeStruct((1, lanes), F32)],
        compiler_params=_cp(1), name="norm2_route",
    )(x_all, g, mods.m4, mods.m4, mods.m3, mods.m3, rw, rb)


def _route_chunk(dest_hbm, dsm, sems, step, n_steps):
    def fetch(s, slot):
        return pltpu.make_async_copy(dest_hbm.at[pl.ds(s * ROUTE_CHUNK, ROUTE_CHUNK)],
                                     dsm.at[pl.ds(slot * ROUTE_CHUNK, ROUTE_CHUNK)], sems.at[slot])

    slot = step % 2

    @pl.when(step == 0)
    def _():
        fetch(0, 0).start()

    fetch(step, slot).wait()

    @pl.when(step + 1 < n_steps)
    def _():
        fetch(step + 1, 1 - slot).start()

    return slot * ROUTE_CHUNK


def _dispatch_body(last_ref, nused_ref, dest_hbm, hp_ref, xs_hbm, dsm, zeros_ref, sem_s, sem_z, sem_r,
                   *, nt, ntiles):
    i = pl.program_id(0)

    def zero_tile(t):
        return pltpu.make_async_copy(zeros_ref, xs_hbm.at[pl.ds(t * MOE_TM, MOE_TM), :], sem_z)

    @pl.when(i == 0)
    def _():
        zeros_ref[...] = jnp.zeros_like(zeros_ref)
        for e in range(N_EXP):
            @pl.when(last_ref[e] >= 0)
            def _():
                zero_tile(last_ref[e]).start()

        def fill(t, c):
            zero_tile(t).start()
            return c

        lax.fori_loop(nused_ref[0], ntiles, fill, 0)
        for e in range(N_EXP):
            @pl.when(last_ref[e] >= 0)
            def _():
                zero_tile(0).wait()

        def fill_wait(t, c):
            zero_tile(0).wait()
            return c

        lax.fori_loop(nused_ref[0], ntiles, fill_wait, 0)

    slot = _route_chunk(dest_hbm, dsm, sem_s, i, pl.num_programs(0))
    rows = jnp.minimum(TOK_CHUNK, nt - i * TOK_CHUNK)

    def row_copy(r, d):
        return pltpu.make_async_copy(hp_ref.at[pl.ds(r, 1), :], xs_hbm.at[pl.ds(d, 1), :], sem_r)

    def issue(r, c):
        for k in range(TOPK):
            row_copy(r, dsm[slot + r * TOPK + k]).start()
        return c

    lax.fori_loop(0, rows, issue, 0)

    def drain(r, c):
        for k in range(TOPK):
            row_copy(0, 0).wait()
        return c

    lax.fori_loop(0, rows, drain, 0)


def _dispatch(hp, dest_pad, last_tile, nused, ntiles):
    nt = hp.shape[0]
    steps = dest_pad.shape[0] // ROUTE_CHUNK
    return pl.pallas_call(
        functools.partial(_dispatch_body, nt=nt, ntiles=ntiles),
        grid_spec=pltpu.PrefetchScalarGridSpec(
            num_scalar_prefetch=2,
            grid=(steps,),
            in_specs=[pl.BlockSpec(memory_space=pl.ANY),
                      pl.BlockSpec((TOK_CHUNK, D // 2), lambda i, last, nu: (i, 0))],
            out_specs=pl.BlockSpec(memory_space=pl.ANY),
            scratch_shapes=[pltpu.SMEM((2 * ROUTE_CHUNK,), I32),
                            pltpu.VMEM((MOE_TM, D // 2), U32),
                            pltpu.SemaphoreType.DMA((2,)), pltpu.SemaphoreType.DMA(()),
                            pltpu.SemaphoreType.DMA(())]),
        out_shape=jax.ShapeDtypeStruct((ntiles * MOE_TM, D // 2), U32),
        compiler_params=_cp(1), name="moe_dispatch",
    )(last_tile, nused, dest_pad, hp)


def _expert_changed(texp_ref, t):
    prev = texp_ref[jnp.maximum(t - 1, 0)]
    return (t == 0) | (texp_ref[t] != prev)


def _by_valid_rows(valid, out_ref, compute_rows):
    half = MOE_TM // 2

    @pl.when(valid > half)
    def _():
        compute_rows(MOE_TM)

    @pl.when((valid > 0) & (valid <= half))
    def _():
        compute_rows(half)
        out_ref[half:] = jnp.zeros_like(out_ref[half:])

    @pl.when(valid == 0)
    def _():
        out_ref[...] = jnp.zeros_like(out_ref)


def _gate_up_body(texp_ref, nused_ref, tvalid_ref, x_ref, wg_ref, wu_ref, bg_ref, bu_ref, act_ref, wg_bf, wu_bf):
    del nused_ref
    t = pl.program_id(1)
    valid = tvalid_ref[t]

    @pl.when((valid > 0) & _expert_changed(texp_ref, t))
    def _():
        wg_bf[...] = wg_ref[...].astype(BF16)
        wu_bf[...] = wu_ref[...].astype(BF16)

    def swiglu_rows(rows):
        x = _unpack_halves(x_ref[:rows])
        gate = jnp.dot(x, wg_bf[...], preferred_element_type=F32) + bg_ref[...]
        up = jnp.dot(x, wu_bf[...], preferred_element_type=F32) + bu_ref[...]
        gate = jnp.minimum(gate, SW_LIMIT)
        up = jnp.clip(up, -SW_LIMIT, SW_LIMIT)
        act_ref[:rows] = ((up + 1.0) * gate * jax.nn.sigmoid(SW_ALPHA * gate)).astype(act_ref.dtype)

    _by_valid_rows(valid, act_ref, swiglu_rows)


def _gate_up(xs, texp, nused, tvalid, w_gate_up, b_gate_up, layer):
    p_rows = xs.shape[0]
    ntiles = p_rows // MOE_TM
    tf = 1024
    nf = DFF // tf
    tile = lambda t, nu: jnp.minimum(t, nu[0] - 1)
    bgu = b_gate_up.reshape(N_LAYERS, N_EXP, 1, 2 * DFF)
    wspec = lambda off: pl.BlockSpec((None, None, D, tf),
                                     lambda f, t, te, nu, tv: (layer, te[tile(t, nu)], 0, off + f))
    bspec = lambda off: pl.BlockSpec((None, None, 1, tf),
                                     lambda f, t, te, nu, tv: (layer, te[tile(t, nu)], 0, off + f))
    return pl.pallas_call(
        _gate_up_body,
        grid_spec=pltpu.PrefetchScalarGridSpec(
            num_scalar_prefetch=3,
            grid=(nf, ntiles),
            in_specs=[pl.BlockSpec((MOE_TM, D // 2), lambda f, t, te, nu, tv: (tile(t, nu), 0)),
                      wspec(0), wspec(nf), bspec(0), bspec(nf)],
            out_specs=pl.BlockSpec((MOE_TM, tf), lambda f, t, te, nu, tv: (t, f)),
            scratch_shapes=[pltpu.VMEM((D, tf), BF16), pltpu.VMEM((D, tf), BF16)]),
        out_shape=jax.ShapeDtypeStruct((p_rows, DFF), BF16),
        compiler_params=_cp(2), name="moe_gate_up",
    )(texp, nused, tvalid, xs, w_gate_up, w_gate_up, bgu, bgu)


def _down_body(texp_ref, nused_ref, tvalid_ref, a_ref, w_ref, b_ref, y_ref, w_bf):
    del nused_ref
    t = pl.program_id(1)
    valid = tvalid_ref[t]

    @pl.when((valid > 0) & _expert_changed(texp_ref, t))
    def _():
        w_bf[...] = w_ref[...].astype(BF16)

    def down_rows(rows):
        y_ref[:rows] = jnp.dot(a_ref[:rows], w_bf[...], preferred_element_type=F32) + b_ref[...]

    _by_valid_rows(valid, y_ref, down_rows)


def _down(act, texp, nused, tvalid, w_down, b_down, layer):
    p_rows = act.shape[0]
    ntiles = p_rows // MOE_TM
    tn = DOWN_TN
    tile = lambda t, nu: jnp.minimum(t, nu[0] - 1)
    bd = b_down.reshape(N_LAYERS, N_EXP, 1, D)
    return pl.pallas_call(
        _down_body,
        grid_spec=pltpu.PrefetchScalarGridSpec(
            num_scalar_prefetch=3,
            grid=(D // tn, ntiles),
            in_specs=[pl.BlockSpec((MOE_TM, DFF), lambda n, t, te, nu, tv: (tile(t, nu), 0)),
                      pl.BlockSpec((None, None, DFF, tn),
                                   lambda n, t, te, nu, tv: (layer, te[tile(t, nu)], 0, n)),
                      pl.BlockSpec((None, None, 1, tn),
                                   lambda n, t, te, nu, tv: (layer, te[tile(t, nu)], 0, n))],
            out_specs=pl.BlockSpec((MOE_TM, tn), lambda n, t, te, nu, tv: (t, n)),
            scratch_shapes=[pltpu.VMEM((DFF, tn), BF16)]),
        out_shape=jax.ShapeDtypeStruct((p_rows, D), F32),
        compiler_params=_cp(2), name="moe_down",
    )(texp, nused, tvalid, act, w_down, bd)


def _combine_body(dest_hbm, ys_hbm, x_ref, wt_ref, gp_ref, gs_ref, *rest, ns, last):
    if last:
        g_ref, yp_ref, ysm_ref, dsm, buf, sem_s, sem_r = rest
    else:
        g_ref, shp_ref, scp_ref, shs_ref, scs_ref, o_ref, h_ref, dsm, buf, sem_s, sem_r = rest
    i = pl.program_id(0)
    slot = _route_chunk(dest_hbm, dsm, sem_s, i, pl.num_programs(0))

    def row_copy(r, k, d):
        return pltpu.make_async_copy(ys_hbm.at[pl.ds(d, 1), :], buf.at[k, pl.ds(r, 1), :], sem_r)

    def combined(rows, gate_ref):
        def issue(r, c):
            for k in range(TOPK):
                row_copy(r, k, dsm[slot + r * TOPK + k]).start()
            return c

        lax.fori_loop(0, rows, issue, 0)

        def drain(r, c):
            for k in range(TOPK):
                row_copy(0, 0, 0).wait()
            return c

        lax.fori_loop(0, rows, drain, 0)
        wt = wt_ref[:rows]
        moe = buf[0, :rows] * wt[:, 0:1]
        for k in range(1, TOPK):
            moe = moe + buf[k, :rows] * wt[:, k:k + 1]
        return x_ref[:rows] + gate_ref[...] * moe

    def prompt():
        x_new = combined(x_ref.shape[0], gp_ref)
        if last:
            yp_ref[...] = _normed(x_new, g_ref[...])
        else:
            o_ref[...] = x_new
            h_ref[...] = (_normed(x_new, g_ref[...]) * (1.0 + scp_ref[...]) + shp_ref[...]).astype(h_ref.dtype)

    def sample():
        x_new = combined(ns, gs_ref)
        if last:
            ysm_ref[...] = _normed(x_new, g_ref[...])
        else:
            o_ref[:ns] = x_new
            o_ref[ns:] = jnp.zeros_like(o_ref[ns:])
            h_ref[:ns] = (_normed(x_new, g_ref[...]) * (1.0 + scs_ref[...]) + shs_ref[...]).astype(h_ref.dtype)
            h_ref[ns:] = jnp.zeros_like(h_ref[ns:])

    _prompt_or_sample(i, pl.num_programs(0), prompt, sample)


def _combine(x_all, ys, dest_pad, wts, mods, layer, np_rows, ns, next_norm):
    nt = x_all.shape[0]
    assert np_rows % TOK_CHUNK == 0 and ns <= TOK_CHUNK
    npt = np_rows // TOK_CHUNK
    last = next_norm[1] is None
    any_spec = pl.BlockSpec(memory_space=pl.ANY)
    gp, gs = mods.specs(layer, 2, TOK_CHUNK)
    rows = lambda w: pl.BlockSpec((TOK_CHUNK, w), lambda i: (i, 0))
    in_specs = [any_spec, any_spec, rows(D), rows(TOPK), gp, gs]
    args = [dest_pad, ys, x_all, wts, *mods.args]
    if last:
        in_specs.append(pl.BlockSpec((1, D), lambda i: (0, 0)))
        args.append(next_norm[0].reshape(1, D))
        out_specs = [pl.BlockSpec((TOK_CHUNK, D), lambda i: (jnp.minimum(i, npt - 1), 0)),
                     pl.BlockSpec((ns, D), lambda i: (0, 0))]
        out_shape = [jax.ShapeDtypeStruct((np_rows, D), F32), jax.ShapeDtypeStruct((ns, D), F32)]
    else:
        g1, mods1 = next_norm
        shp, shs = mods1.specs(layer + 1, 0, TOK_CHUNK)
        scp, scs = mods1.specs(layer + 1, 1, TOK_CHUNK)
        in_specs += [pl.BlockSpec((None, 1, D), lambda i: (layer + 1, 0, 0)), shp, scp, shs, scs]
        args += [g1, mods1.m4, mods1.m4, mods1.m3, mods1.m3]
        out_specs = [rows(D), rows(D)]
        out_shape = [jax.ShapeDtypeStruct((nt, D), F32), jax.ShapeDtypeStruct((nt, D), BF16)]
    return pl.pallas_call(
        functools.partial(_combine_body, ns=ns, last=last),
        grid=(npt + 1,),
        in_specs=in_specs,
        out_specs=out_specs,
        out_shape=out_shape,
        scratch_shapes=[pltpu.SMEM((2 * ROUTE_CHUNK,), I32), pltpu.VMEM((TOPK, TOK_CHUNK, D), F32),
                        pltpu.SemaphoreType.DMA((2,)), pltpu.SemaphoreType.DMA(())],
        compiler_params=_cp(1), name="moe_combine",
    )(*args)


def _moe(x_all, g, mods, layer, router_w, router_b, w_gate_up, b_gate_up, w_down, b_down, np_rows, ns,
         next_norm):
    nt = x_all.shape[0]
    hp, idx, wts, rank, cnt = _route(x_all, g, mods, layer, router_w, router_b, np_rows, ns)
    counts = cnt[0, :N_EXP].astype(I32)
    padded = ((counts + MOE_TM - 1) // MOE_TM) * MOE_TM
    ends = jnp.cumsum(padded)
    gstart = ends - padded
    dest = gstart[idx] + rank
    n_entries = nt * TOPK
    n_chunks = -(-n_entries // ROUTE_CHUNK)
    dest_pad = jnp.zeros((n_chunks * ROUTE_CHUNK,), I32).at[:n_entries].set(dest.reshape(-1))
    ntiles = -(-n_entries // MOE_TM) + N_EXP
    tile_ends = ends // MOE_TM
    tile_ids = jnp.arange(ntiles, dtype=I32)
    texp = jnp.minimum(jnp.sum((tile_ids[:, None] >= tile_ends[None, :]).astype(I32), axis=1), N_EXP - 1)
    nused = tile_ends[-1:].astype(I32)
    last_tile = jnp.where(padded > 0, tile_ends - 1, -1).astype(I32)
    tvalid = jnp.clip((gstart + counts)[texp] - tile_ids * MOE_TM, 0, MOE_TM)
    tvalid = jnp.where(tile_ids < nused[0], tvalid, 0).astype(I32)
    xs = _dispatch(hp, dest_pad, last_tile, nused, ntiles)
    act = _gate_up(xs, texp, nused, tvalid, w_gate_up, b_gate_up, layer)
    ys = _down(act, texp, nused, tvalid, w_down, b_down, layer)
    return _combine(x_all, ys, dest_pad, wts, mods, layer, np_rows, ns, next_norm)


def kernel(x_prompt, x_sample, state_ret, cache_kv_w128_d1, cache_kv_w512_d4, cache_kv_w2048_d16, c_prompt, c_sample, norm1_g, ada1_w, ada1_b, w_in, w_out_ret, w_out_dil, w_out, norm2_g, ada2_w, ada2_b, router_w, router_b, w_gate_up, b_gate_up, w_down, b_down, final_norm_g):
    nb, seq, _ = x_prompt.shape
    ns, dec_seq, _ = x_sample.shape
    assert dec_seq == 1 and seq == 2048 and ns % 16 == 0
    past = cache_kv_w2048_d16.shape[2]
    np_rows = nb * seq
    assert np_rows % ns == 0
    caches = (cache_kv_w128_d1, cache_kv_w512_d4, cache_kv_w2048_d16)

    x_src = (x_prompt.reshape(np_rows, D), x_sample.reshape(ns, D), 0)
    nc = -(-(ns + nb) // 16) * 16
    c_all = jnp.zeros((nc, D), F32).at[:ns].set(c_sample).at[ns:ns + nb].set(c_prompt)
    mods1 = _Mods(_mods(c_all, ada1_w, ada1_b), ns, nb, seq)
    mods2 = _Mods(_mods(c_all, ada2_w, ada2_b), ns, nb, seq)
    g1 = norm1_g.reshape(N_LAYERS, 1, D)
    g2 = norm2_g.reshape(N_LAYERS, 1, D)

    ret_p, ret_s, projs, kv_s = [], [], [], [[] for _ in DIL]
    h = _norm_mod(x_src, g1, mods1, 0, np_rows, ns)
    for layer in range(N_LAYERS):
        proj = _proj(h, w_in, layer)
        projs.append(proj)
        ya_p, st_p = _ret_prompt(proj, nb, seq)
        ya_s, st_s = _ret_sample(proj, state_ret, layer, np_rows, ns, past)
        yb_p = _dil_prompt(proj, nb, seq)
        yb_s = _dil_sample(proj, caches, layer, np_rows, ns)
        x_all = _mix_out(x_src, ya_p, yb_p, ya_s, yb_s, proj, w_out_ret, w_out_dil, w_out, mods1, layer)
        last = layer == N_LAYERS - 1
        x_all, h = _moe(x_all, g2, mods2, layer, router_w, router_b, w_gate_up, b_gate_up, w_down, b_down,
                        np_rows, ns, (final_norm_g, None) if last else (g1, mods1))
        x_src = (x_all, x_all, np_rows // ns)
        ret_p.append(st_p)
        ret_s.append(st_s)
        for g in range(len(DIL)):
            c0 = OFF_DIL + g * 3 * DIL_W + DIL_W
            kv_s[g].append(proj[np_rows:, c0:c0 + 2 * DIL_W].reshape(ns, 1, 2, DIL_H, DIL_E))
    y_p, y_s = x_all, h
    kv_p = [_kv_out(projs, g, nb, seq) for g in range(len(DIL))]
    return (y_p.reshape(nb, seq, D), y_s.reshape(ns, 1, D),
            jnp.stack(ret_p), kv_p[0], kv_p[1], kv_p[2],
            jnp.stack(ret_s), jnp.stack(kv_s[0]), jnp.stack(kv_s[1]), jnp.stack(kv_s[2]))
```

```python
import functools

import jax
import jax.numpy as jnp
from jax import lax
from jax.experimental import pallas as pl
from jax.experimental.pallas import tpu as pltpu

F32 = jnp.float32
BF16 = jnp.bfloat16
U32 = jnp.uint32
I32 = jnp.int32

D = 2048
N_LAYERS = 2
RET_H = 8
RET_DK = 128
RET_DV = 256
RET_C = 128
ROPE_BASE = 10000.0
DIL = ((128, 1), (512, 4), (2048, 16))
DIL_H = 4
DIL_E = 128
DIL_W = DIL_H * DIL_E
N_EXP = 32
TOPK = 4
DFF = 2048
SW_LIMIT = 7.0
SW_ALPHA = 1.702
EPS = 1e-6

OFF_Q = 0
OFF_K = 1024
OFF_V = 2048
OFF_G = 4096
OFF_DIL = 6144
OFF_GA = 10752
OFF_GB = 12800
IN_W = 14848

VMEM_LIMIT = 56 * 1024 * 1024
MOE_TM = 512
DOWN_TN = 2048
ROUTE_CHUNK = 2048
TOK_CHUNK = ROUTE_CHUNK // TOPK


def _cp(n_grid_dims):
    return pltpu.CompilerParams(
        dimension_semantics=("arbitrary",) * n_grid_dims,
        vmem_limit_bytes=VMEM_LIMIT)


def _row_tile(n, cap):
    for k in range(1, n + 1):
        if n % k == 0 and n // k <= cap and (n // k) % 16 == 0:
            return n // k
    raise ValueError((n, cap))


def _silu(x):
    return x * jax.nn.sigmoid(x)


def _mods_body(c_ref, w_ref, b_ref, o_ref):
    a = _silu(c_ref[...]).astype(BF16)
    o_ref[...] = jnp.dot(a, w_ref[...].astype(BF16), preferred_element_type=F32) + b_ref[...]


def _mods(c_all, ada_w, ada_b):
    nc = c_all.shape[0]
    tn = 1024
    return pl.pallas_call(
        _mods_body,
        grid=(N_LAYERS, 3 * D // tn),
        in_specs=[pl.BlockSpec((nc, D), lambda l, n: (0, 0)),
                  pl.BlockSpec((None, D, tn), lambda l, n: (l, 0, n)),
                  pl.BlockSpec((None, 1, tn), lambda l, n: (l, 0, n))],
        out_specs=pl.BlockSpec((None, nc, tn), lambda l, n: (l, 0, n)),
        out_shape=jax.ShapeDtypeStruct((N_LAYERS, nc, 3 * D), F32),
        compiler_params=_cp(2), name="ada_mods",
    )(c_all, ada_w, ada_b.reshape(N_LAYERS, 1, 3 * D))


class _Mods:
    def __init__(self, m, ns, nb, seq):
        self.m3 = m
        self.m4 = m.reshape(m.shape[0], m.shape[1], 1, 3 * D)
        self.ns, self.nb, self.seq = ns, nb, seq

    @property
    def args(self):
        return (self.m4, self.m3)

    def specs(self, layer, which, tr, tn=D, col_axis=False):
        per = self.seq // tr
        ns, nb = self.ns, self.nb
        nblk = D // tn
        if col_axis:
            p_map = lambda n, i: (layer, ns + jnp.minimum(i // per, nb - 1), 0, which * nblk + n)
            s_map = lambda n, i: (layer, 0, which * nblk + n)
        else:
            p_map = lambda i: (layer, ns + jnp.minimum(i // per, nb - 1), 0, which * nblk)
            s_map = lambda i: (layer, 0, which * nblk)
        return [pl.BlockSpec((None, None, 1, tn), p_map), pl.BlockSpec((None, ns, tn), s_map)]


def _prompt_or_sample(step, n_steps, prompt_fn, sample_fn):
    pl.when(step < n_steps - 1)(prompt_fn)
    pl.when(step == n_steps - 1)(sample_fn)


def _normed(x, g):
    return x * lax.rsqrt(jnp.mean(x * x, axis=-1, keepdims=True) + EPS) * g


def _norm_mod_body(xp_ref, xs_ref, g_ref, shp_ref, scp_ref, shs_ref, scs_ref, o_ref, *, ns):
    def prompt():
        y = _normed(xp_ref[...], g_ref[...])
        o_ref[...] = (y * (1.0 + scp_ref[...]) + shp_ref[...]).astype(o_ref.dtype)

    def sample():
        y = _normed(xs_ref[...], g_ref[...])
        o_ref[:ns] = (y * (1.0 + scs_ref[...]) + shs_ref[...]).astype(o_ref.dtype)
        o_ref[ns:] = jnp.zeros_like(o_ref[ns:])

    _prompt_or_sample(pl.program_id(0), pl.num_programs(0), prompt, sample)


def _norm_mod(x_src, g, mods, layer, np_rows, ns):
    x_main, x_samp, sblk = x_src
    tr = 1024
    npt = np_rows // tr
    shp, shs = mods.specs(layer, 0, tr)
    scp, scs = mods.specs(layer, 1, tr)
    return pl.pallas_call(
        functools.partial(_norm_mod_body, ns=ns),
        grid=(npt + 1,),
        in_specs=[pl.BlockSpec((tr, D), lambda i: (jnp.minimum(i, npt - 1), 0)),
                  pl.BlockSpec((ns, D), lambda i: (sblk, 0)),
                  pl.BlockSpec((None, 1, D), lambda i: (layer, 0, 0)),
                  shp, scp, shs, scs],
        out_specs=pl.BlockSpec((tr, D), lambda i: (i, 0)),
        out_shape=jax.ShapeDtypeStruct((np_rows + ns, D), BF16),
        compiler_params=_cp(1), name="norm1",
    )(x_main, x_samp, g, mods.m4, mods.m4, mods.m3, mods.m3)


def _proj_body(x_ref, w_ref, o_ref, wbf_ref):
    @pl.when(pl.program_id(1) == 0)
    def _():
        wbf_ref[...] = w_ref[...].astype(BF16)

    o_ref[...] = jnp.dot(x_ref[...], wbf_ref[...], preferred_element_type=F32).astype(o_ref.dtype)


def _proj(h, w_in, layer):
    nt = h.shape[0]
    tm = _row_tile(nt, 1376)
    tn = 1024
    return pl.pallas_call(
        _proj_body,
        grid=(pl.cdiv(IN_W, tn), nt // tm),
        in_specs=[pl.BlockSpec((tm, D), lambda n, m: (m, 0)),
                  pl.BlockSpec((None, D, tn), lambda n, m: (layer, 0, n))],
        out_specs=pl.BlockSpec((tm, tn), lambda n, m: (m, n)),
        out_shape=jax.ShapeDtypeStruct((nt, IN_W), F32),
        scratch_shapes=[pltpu.VMEM((D, tn), BF16)],
        compiler_params=_cp(2), name="in_proj",
    )(h, w_in)


def _rope_tables(pos):
    half = RET_DK // 2
    inv = ROPE_BASE ** -jnp.linspace(0.0, 1.0, half, dtype=F32)
    ang = pos.astype(F32)[:, None] * inv[None, :]
    cos = jnp.cos(ang)
    sin = jnp.sin(ang)
    return jnp.concatenate([cos, cos], axis=-1), jnp.concatenate([-sin, sin], axis=-1)


def _decay_tables(chunk):
    log_g = jnp.log1p(-jnp.exp2(-5.0 - jnp.arange(RET_H, dtype=F32)))
    idx = jnp.arange(chunk, dtype=F32)
    rel = idx[:, None] - idx[None, :]
    inner = jnp.where(rel >= 0, jnp.exp(log_g[:, None, None] * jnp.maximum(rel, 0.0)), 0.0)
    q_decay = jnp.exp(log_g[:, None] * (idx + 1.0)[None, :])
    k_decay = jnp.exp(log_g[:, None] * (chunk - 1.0 - idx)[None, :])
    chunk_decay = jnp.exp(log_g * chunk)
    return inner, q_decay, k_decay, chunk_decay


def _rot(x, cos, sin):
    return x * cos + pltpu.roll(x, RET_DK // 2, 1) * sin


def _head_norm_gate(o, g):
    on = o * lax.rsqrt(jnp.mean(o * o, axis=-1, keepdims=True) + EPS)
    return _silu(g) * on


def _ret_prompt_body(q_ref, k_ref, v_ref, g_ref, cos_ref, sin_ref, inner_ref, qd_ref, kd_ref, cd_ref,
                     ya_ref, s_ref):
    @pl.when(pl.program_id(1) == 0)
    def _():
        s_ref[...] = jnp.zeros_like(s_ref)

    for c in range(RET_STEP // RET_C):
        rows = slice(c * RET_C, (c + 1) * RET_C)
        cos = cos_ref[rows, :]
        sin = sin_ref[rows, :]
        for h in range(RET_H):
            q = _rot(q_ref[rows, h * RET_DK:(h + 1) * RET_DK], cos, sin)
            k = _rot(k_ref[rows, h * RET_DK:(h + 1) * RET_DK], cos, sin) * (RET_DK ** -0.5)
            vb = v_ref[rows, h * RET_DV:(h + 1) * RET_DV].astype(BF16)
            s = lax.dot_general(q.astype(BF16), k.astype(BF16), (((1,), (1,)), ((), ())),
                                preferred_element_type=F32) * inner_ref[h]
            state = s_ref[h]
            o = (jnp.dot(s.astype(BF16), vb, preferred_element_type=F32)
                 + jnp.dot((q * qd_ref[h]).astype(BF16), state.astype(BF16), preferred_element_type=F32))
            kt = (k * kd_ref[h]).T.astype(BF16)
            s_ref[h] = state * cd_ref[h] + jnp.dot(kt, vb, preferred_element_type=F32)
            y = _head_norm_gate(o, g_ref[rows, h * RET_DV:(h + 1) * RET_DV])
            ya_ref[rows, h * RET_DV:(h + 1) * RET_DV] = y.astype(ya_ref.dtype)


RET_STEP = 2 * RET_C


def _ret_prompt(proj, nb, seq):
    nch = seq // RET_STEP
    cos, sin = _rope_tables(jnp.arange(seq, dtype=I32))
    inner, qd, kd, cd = _decay_tables(RET_C)
    qd_b = jnp.broadcast_to(qd[:, :, None], (RET_H, RET_C, RET_DK))
    kd_b = jnp.broadcast_to(kd[:, :, None], (RET_H, RET_C, RET_DK))
    cd_b = jnp.broadcast_to(cd[:, None, None], (RET_H, 1, RET_DV))
    row = lambda b, c: b * nch + c
    full3 = lambda shp: pl.BlockSpec(shp, lambda b, c: (0, 0, 0))
    return pl.pallas_call(
        _ret_prompt_body,
        grid=(nb, nch),
        in_specs=[pl.BlockSpec((RET_STEP, 1024), lambda b, c: (row(b, c), OFF_Q // 1024)),
                  pl.BlockSpec((RET_STEP, 1024), lambda b, c: (row(b, c), OFF_K // 1024)),
                  pl.BlockSpec((RET_STEP, 2048), lambda b, c: (row(b, c), OFF_V // 2048)),
                  pl.BlockSpec((RET_STEP, 2048), lambda b, c: (row(b, c), OFF_G // 2048)),
                  pl.BlockSpec((RET_STEP, RET_DK), lambda b, c: (c, 0)),
                  pl.BlockSpec((RET_STEP, RET_DK), lambda b, c: (c, 0)),
                  full3((RET_H, RET_C, RET_C)), full3((RET_H, RET_C, RET_DK)),
                  full3((RET_H, RET_C, RET_DK)), full3((RET_H, 1, RET_DV))],
        out_specs=[pl.BlockSpec((RET_STEP, D), lambda b, c: (row(b, c), 0)),
                   pl.BlockSpec((None, RET_H, RET_DK, RET_DV), lambda b, c: (b, 0, 0, 0))],
        out_shape=[jax.ShapeDtypeStruct((nb * seq, D), BF16),
                   jax.ShapeDtypeStruct((nb, RET_H, RET_DK, RET_DV), F32)],
        compiler_params=_cp(2), name="retention_prompt",
    )(proj, proj, proj, proj, cos, sin, inner, qd_b, kd_b, cd_b)


RS_NB = 8


def _ret_sample_body(q_ref, k_ref, v_ref, g_ref, cos_ref, sin_ref, inner_ref, qd_ref, kd_ref, cd_ref,
                     st_ref, ya_ref, so_ref):
    i = pl.program_id(0)
    cos = cos_ref[...]
    sin = sin_ref[...]
    row_i = lax.broadcasted_iota(I32, (RET_DK, RET_DK), 0)
    col_i = lax.broadcasted_iota(I32, (RET_DK, RET_DK), 1)
    eye = row_i == col_i
    rows = pl.ds(pl.multiple_of(i * RS_NB, RS_NB), RS_NB)
    q_all = q_ref[rows, :]
    k_all = k_ref[rows, :]
    v_all = v_ref[rows, :]
    g_all = g_ref[rows, :]
    out_rows = []
    for j in range(RS_NB):
        heads = []
        for h in range(RET_H):
            q = _rot(q_all[j:j + 1, h * RET_DK:(h + 1) * RET_DK], cos, sin)
            k = _rot(k_all[j:j + 1, h * RET_DK:(h + 1) * RET_DK], cos, sin) * (RET_DK ** -0.5)
            v = v_all[j:j + 1, h * RET_DV:(h + 1) * RET_DV]
            s = jnp.sum(q * k, axis=-1, keepdims=True) * inner_ref[h]
            state = st_ref[j, h]
            q8 = jnp.broadcast_to(q * qd_ref[h], (8, RET_DK)).astype(BF16)
            qs = jnp.dot(q8, state.astype(BF16), preferred_element_type=F32)[0:1]
            o = s * v + qs
            kdiag = jnp.where(eye, jnp.broadcast_to(k * kd_ref[h], (RET_DK, RET_DK)), 0.0).astype(BF16)
            vb = jnp.broadcast_to(v, (RET_DK, RET_DV)).astype(BF16)
            so_ref[j, h] = state * cd_ref[h] + jnp.dot(kdiag, vb, preferred_element_type=F32)
            heads.append(_head_norm_gate(o, g_all[j:j + 1, h * RET_DV:(h + 1) * RET_DV]))
        out_rows.append(jnp.concatenate(heads, axis=1))
    ya_ref[...] = jnp.concatenate(out_rows, axis=0)


def _ret_sample(proj, state_ret, layer, np_rows, ns, past):
    cos, sin = _rope_tables(past + jnp.arange(1, dtype=I32))
    inner, qd, kd, cd = _decay_tables(1)
    inner_b = inner.reshape(RET_H, 1, 1)
    qd_b = jnp.broadcast_to(qd[:, :, None], (RET_H, 1, RET_DK))
    kd_b = jnp.broadcast_to(kd[:, :, None], (RET_H, 1, RET_DK))
    cd_b = jnp.broadcast_to(cd[:, None, None], (RET_H, 1, RET_DV))
    sblk = np_rows // ns
    full = lambda shp: pl.BlockSpec(shp, lambda i: (0,) * len(shp))
    return pl.pallas_call(
        _ret_sample_body,
        grid=(ns // RS_NB,),
        in_specs=[pl.BlockSpec((ns, 1024), lambda i: (sblk, OFF_Q // 1024)),
                  pl.BlockSpec((ns, 1024), lambda i: (sblk, OFF_K // 1024)),
                  pl.BlockSpec((ns, 2048), lambda i: (sblk, OFF_V // 2048)),
                  pl.BlockSpec((ns, 2048), lambda i: (sblk, OFF_G // 2048)),
                  full((1, RET_DK)), full((1, RET_DK)),
                  full((RET_H, 1, 1)), full((RET_H, 1, RET_DK)), full((RET_H, 1, RET_DK)),
                  full((RET_H, 1, RET_DV)),
                  pl.BlockSpec((None, RS_NB, RET_H, RET_DK, RET_DV), lambda i: (layer, i, 0, 0, 0))],
        out_specs=[pl.BlockSpec((RS_NB, D), lambda i: (i, 0)),
                   pl.BlockSpec((RS_NB, RET_H, RET_DK, RET_DV), lambda i: (i, 0, 0, 0))],
        out_shape=[jax.ShapeDtypeStruct((ns, D), F32),
                   jax.ShapeDtypeStruct((ns, RET_H, RET_DK, RET_DV), F32)],
        compiler_params=_cp(1), name="retention_sample",
    )(proj, proj, proj, proj, cos, sin, inner_b, qd_b, kd_b, cd_b, state_ret)


def _merge_groups(parts):
    m_all = parts[0][1]
    for _, m_g, _ in parts[1:]:
        m_all = jnp.maximum(m_all, m_g)
    num = 0.0
    den = 0.0
    for n_g, m_g, l_g in parts:
        w = jnp.exp(m_g - m_all)
        num = num + n_g * w
        den = den + l_g * w
    return num / den


def _dil_prompt_body(*refs, seq):
    qkv = refs[:9]
    yb_ref = refs[9]
    num_ref, m_ref, l_ref = refs[10:13]
    band = 128
    scale = DIL_E ** -0.5
    qi = lax.broadcasted_iota(I32, (band, band), 0)
    kj1 = lax.broadcasted_iota(I32, (band, band), 1)
    qi2 = lax.broadcasted_iota(I32, (band, 2 * band), 0)
    kj2 = lax.broadcasted_iota(I32, (band, 2 * band), 1)
    first_valid = kj1 <= qi
    later_valid = (kj2 >= qi2) & (kj2 <= qi2 + band)
    for g, (window, dil) in enumerate(DIL):
        assert window // dil == band
        q_ref, k_ref, v_ref = qkv[3 * g:3 * g + 3]
        nblk = seq // dil // band
        for p in range(dil):
            for n in range(nblk):
                rows_q = pl.ds(p + n * band * dil, band, stride=dil)
                qb = q_ref[rows_q, :].astype(BF16)
                if n == 0:
                    rows_k = rows_q
                    valid = first_valid
                else:
                    rows_k = pl.ds(p + (n - 1) * band * dil, 2 * band, stride=dil)
                    valid = later_valid
                kk = k_ref[rows_k, :].astype(BF16)
                vv = v_ref[rows_k, :].astype(BF16)
                s = lax.dot_general(qb, kk, (((1,), (1,)), ((), ())), preferred_element_type=F32) * scale
                s = jnp.where(valid, s, -jnp.inf)
                m = jnp.max(s, axis=-1, keepdims=True)
                pe = jnp.exp(s - m)
                l = jnp.sum(pe, axis=-1, keepdims=True)
                num_ref[g, rows_q, :] = jnp.dot(pe.astype(BF16), vv, preferred_element_type=F32)
                m_ref[g, rows_q, :] = jnp.broadcast_to(m, (band, DIL_E))
                l_ref[g, rows_q, :] = jnp.broadcast_to(l, (band, DIL_E))
    parts = [(num_ref[g], m_ref[g], l_ref[g]) for g in range(len(DIL))]
    yb_ref[...] = _merge_groups(parts).astype(yb_ref.dtype)


def _dil_prompt(proj, nb, seq):
    in_specs = []
    for g in range(len(DIL)):
        for j in range(3):
            cb = (OFF_DIL + g * 3 * DIL_W + j * DIL_W) // DIL_E
            in_specs.append(pl.BlockSpec((seq, DIL_E), lambda b, h, cb=cb: (b, cb + h)))
    scr = pltpu.VMEM((len(DIL), seq, DIL_E), F32)
    return pl.pallas_call(
        functools.partial(_dil_prompt_body, seq=seq),
        grid=(nb, DIL_H),
        in_specs=in_specs,
        out_specs=pl.BlockSpec((seq, DIL_E), lambda b, h: (b, h)),
        out_shape=jax.ShapeDtypeStruct((nb * seq, DIL_W), BF16),
        scratch_shapes=[scr, scr, scr],
        compiler_params=_cp(2), name="dilated_prompt",
    )(*([proj] * 9))


DS_NB = 8


def _dil_sample_body(c0_ref, c1_ref, c2_ref, q0_ref, q1_ref, q2_ref, yb_ref):
    i = pl.program_id(0)
    caches = (c0_ref, c1_ref, c2_ref)
    scale = DIL_E ** -0.5
    rows = pl.ds(pl.multiple_of(i * DS_NB, DS_NB), DS_NB)
    news = [ref[rows, :] for ref in (q0_ref, q1_ref, q2_ref)]
    out_rows = []
    for j in range(DS_NB):
        parts = []
        for g in range(len(DIL)):
            new = news[g][j:j + 1]
            q4 = jnp.concatenate([new[:, h * DIL_E:(h + 1) * DIL_E] for h in range(DIL_H)], axis=0)
            k4 = jnp.concatenate([new[:, DIL_W + h * DIL_E:DIL_W + (h + 1) * DIL_E] for h in range(DIL_H)], axis=0)
            v4 = jnp.concatenate([new[:, 2 * DIL_W + h * DIL_E:2 * DIL_W + (h + 1) * DIL_E] for h in range(DIL_H)],
                                 axis=0)
            kc = caches[g][j, :, 0:DIL_H, :]
            vc = caches[g][j, :, DIL_H:2 * DIL_H, :]
            s_c = jnp.sum(kc * q4[None], axis=-1, keepdims=True) * scale
            s_n = jnp.sum(k4 * q4, axis=-1, keepdims=True) * scale
            m = jnp.maximum(jnp.max(s_c, axis=0), s_n)
            p_c = jnp.exp(s_c - m[None])
            p_n = jnp.exp(s_n - m)
            l = jnp.sum(p_c, axis=0) + p_n
            num = jnp.sum(p_c * vc, axis=0) + p_n * v4
            parts.append((num, m, l))
        y = _merge_groups(parts)
        out_rows.append(jnp.concatenate([y[h:h + 1] for h in range(DIL_H)], axis=1))
    yb_ref[...] = jnp.concatenate(out_rows, axis=0)


def _dil_sample(proj, caches, layer, np_rows, ns):
    sblk = np_rows // ns
    views = []
    in_specs = []
    for cache, (window, dil) in zip(caches, DIL):
        wb = cache.shape[2]
        assert wb == window and wb % dil == 0
        views.append(cache.reshape(cache.shape[0] * ns, wb // dil, dil, 2 * DIL_H, DIL_E))
        in_specs.append(pl.BlockSpec((DS_NB, wb // dil, None, 2 * DIL_H, DIL_E),
                                     lambda i: (layer * (ns // DS_NB) + i, 0, 0, 0, 0)))
    for g in range(len(DIL)):
        in_specs.append(pl.BlockSpec((ns, 3 * DIL_W), lambda i, g=g: (sblk, OFF_DIL // (3 * DIL_W) + g)))
    return pl.pallas_call(
        _dil_sample_body,
        grid=(ns // DS_NB,),
        in_specs=in_specs,
        out_specs=pl.BlockSpec((DS_NB, DIL_W), lambda i: (i, 0)),
        out_shape=jax.ShapeDtypeStruct((ns, DIL_W), F32),
        compiler_params=_cp(1), name="dilated_sample",
    )(*views, proj, proj, proj)


def _kv_out_body(*refs):
    o_ref = refs[-1]
    layer = pl.program_id(0)
    for l in range(N_LAYERS):
        @pl.when(layer == l)
        def _(k_ref=refs[2 * l], v_ref=refs[2 * l + 1]):
            for h in range(DIL_H):
                o_ref[:, 0, h, :] = k_ref[:, h * DIL_E:(h + 1) * DIL_E]
                o_ref[:, 1, h, :] = v_ref[:, h * DIL_E:(h + 1) * DIL_E]


def _kv_out(projs, g, nb, seq):
    window = min(DIL[g][0], seq)
    rows = min(window, 512)
    nr = window // rows
    cb = (OFF_DIL + g * 3 * DIL_W + DIL_W) // DIL_W
    first = (seq - window) // rows
    per_b = seq // rows
    last = (nb - 1) * per_b + first + nr - 1
    in_specs = []
    for l in range(N_LAYERS):
        def rblk(ll, b, r, l=l):
            cur = b * per_b + first + r
            return jnp.where(ll == l, cur, jnp.where(ll < l, first, last))
        in_specs.append(pl.BlockSpec((rows, DIL_W), lambda ll, b, r, f=rblk: (f(ll, b, r), cb)))
        in_specs.append(pl.BlockSpec((rows, DIL_W), lambda ll, b, r, f=rblk: (f(ll, b, r), cb + 1)))
    args = []
    for p in projs:
        args += [p, p]
    return pl.pallas_call(
        _kv_out_body,
        grid=(N_LAYERS, nb, nr),
        in_specs=in_specs,
        out_specs=pl.BlockSpec((None, None, rows, 2, DIL_H, DIL_E), lambda ll, b, r: (ll, b, r, 0, 0, 0)),
        out_shape=jax.ShapeDtypeStruct((N_LAYERS, nb, window, 2, DIL_H, DIL_E), F32),
        compiler_params=_cp(3), name="kv_prompt_out",
    )(*args)


W_CHUNK = 256


def _load_weight_bf16(w_hbm, layer, dst_bf, stage, sems):
    n = dst_bf.shape[0] // W_CHUNK

    def chunk(c):
        return pltpu.make_async_copy(w_hbm.at[layer, pl.ds(c * W_CHUNK, W_CHUNK), :], stage.at[c % 2],
                                     sems.at[c % 2])

    chunk(0).start()
    for c in range(n):
        if c + 1 < n:
            chunk(c + 1).start()
        chunk(c).wait()
        dst_bf[c * W_CHUNK:(c + 1) * W_CHUNK, :] = stage[c % 2].astype(BF16)


GATE_TN = 512


def _mix_out_body(xp_ref, xs_ref, yap_ref, ybp_ref, yas_ref, ybs_ref, *rest, ns, layer):
    n_gate = D // GATE_TN
    ga_refs, gb_refs = rest[:n_gate], rest[n_gate:2 * n_gate]
    gp_ref, gs_ref, wr_hbm, wd_hbm, wo_hbm, o_ref, wr_bf, wd_bf, wo_bf, stage, sems = rest[2 * n_gate:]

    @pl.when(pl.program_id(0) == 0)
    def _():
        _load_weight_bf16(wr_hbm, layer, wr_bf, stage, sems)
        _load_weight_bf16(wd_hbm, layer, wd_bf, stage, sems)
        _load_weight_bf16(wo_hbm, layer, wo_bf, stage, sems)

    def mixed(rows, ya, yb, x, gate):
        parts = []
        for c in range(n_gate):
            cols = slice(c * GATE_TN, (c + 1) * GATE_TN)
            y_a = jnp.dot(ya, wr_bf[:, cols], preferred_element_type=F32)
            y_b = jnp.dot(yb, wd_bf[:, cols], preferred_element_type=F32)
            merged = jax.nn.sigmoid(ga_refs[c][:rows]) * y_a + jax.nn.sigmoid(gb_refs[c][:rows]) * y_b
            parts.append(merged.astype(BF16))
        mix = jnp.dot(jnp.concatenate(parts, axis=1), wo_bf[...], preferred_element_type=F32)
        return x + gate * mix

    def prompt():
        o_ref[...] = mixed(o_ref.shape[0], yap_ref[...], ybp_ref[...], xp_ref[...], gp_ref[...])

    def sample():
        o_ref[:ns] = mixed(ns, yas_ref[...].astype(BF16), ybs_ref[...].astype(BF16), xs_ref[...], gs_ref[...])
        o_ref[ns:] = jnp.zeros_like(o_ref[ns:])

    _prompt_or_sample(pl.program_id(0), pl.num_programs(0), prompt, sample)


def _mix_out(x_src, ya_p, yb_p, ya_s, yb_s, proj, w_out_ret, w_out_dil, w_out, mods, layer):
    x_main, x_samp, sblk = x_src
    nt = proj.shape[0]
    np_rows = ya_p.shape[0]
    ns = ya_s.shape[0]
    tm = 256
    npt = np_rows // tm
    ptile = lambda i: jnp.minimum(i, npt - 1)
    gp, gs = mods.specs(layer, 2, tm)
    any_spec = pl.BlockSpec(memory_space=pl.ANY)
    gate_specs = [pl.BlockSpec((tm, GATE_TN), lambda i, cb=(off // GATE_TN + c): (i, cb))
                  for off in (OFF_GA, OFF_GB) for c in range(D // GATE_TN)]
    return pl.pallas_call(
        functools.partial(_mix_out_body, ns=ns, layer=layer),
        grid=(npt + 1,),
        in_specs=[pl.BlockSpec((tm, D), lambda i: (ptile(i), 0)),
                  pl.BlockSpec((ns, D), lambda i: (sblk, 0)),
                  pl.BlockSpec((tm, D), lambda i: (ptile(i), 0)),
                  pl.BlockSpec((tm, DIL_W), lambda i: (ptile(i), 0)),
                  pl.BlockSpec((ns, D), lambda i: (0, 0)),
                  pl.BlockSpec((ns, DIL_W), lambda i: (0, 0)),
                  *gate_specs, gp, gs, any_spec, any_spec, any_spec],
        out_specs=pl.BlockSpec((tm, D), lambda i: (i, 0)),
        out_shape=jax.ShapeDtypeStruct((nt, D), F32),
        scratch_shapes=[pltpu.VMEM((D, D), BF16), pltpu.VMEM((DIL_W, D), BF16), pltpu.VMEM((D, D), BF16),
                        pltpu.VMEM((2, W_CHUNK, D), F32), pltpu.SemaphoreType.DMA((2,))],
        compiler_params=_cp(1), name="mixer_out",
    )(x_main, x_samp, ya_p, yb_p, ya_s, yb_s, *([proj] * (2 * (D // GATE_TN))), *mods.args,
      w_out_ret, w_out_dil, w_out)


def _pack_halves(h):
    a = lax.bitcast_convert_type(h[:, :D // 2].astype(BF16).astype(F32), U32)
    b = lax.bitcast_convert_type(h[:, D // 2:].astype(BF16).astype(F32), U32)
    return a | (b >> 16)


def _unpack_halves(w):
    a = lax.bitcast_convert_type(w & jnp.uint32(0xFFFF0000), F32)
    b = lax.bitcast_convert_type(w << 16, F32)
    return jnp.concatenate([a.astype(BF16), b.astype(BF16)], axis=1)


def _route_rows(x, g, sh, sc, rw, rb, cnt):
    tr = x.shape[0]
    h = _normed(x, g) * (1.0 + sc) + sh
    logits = jnp.dot(h, rw, precision=lax.Precision.HIGHEST, preferred_element_type=F32) + rb
    lane = lax.broadcasted_iota(I32, logits.shape, 1)
    work = logits
    tops = []
    member = jnp.zeros(logits.shape, F32)
    for _ in range(TOPK):
        m = jnp.max(work, axis=-1, keepdims=True)
        sel = jnp.min(jnp.where(work == m, lane, logits.shape[1]), axis=-1, keepdims=True)
        hit = lane == sel
        work = jnp.where(hit, -jnp.inf, work)
        member = jnp.where(hit, 1.0, member)
        tops.append((m, sel, hit))
    ri = lax.broadcasted_iota(I32, (tr, tr), 0)
    ci = lax.broadcasted_iota(I32, (tr, tr), 1)
    tri = jnp.where(ci < ri, 1.0, 0.0).astype(BF16)
    before = jnp.dot(tri, member.astype(BF16), preferred_element_type=F32) + cnt
    es = [jnp.exp(m - tops[0][0]) for m, _, _ in tops]
    den = es[0]
    for e in es[1:]:
        den = den + e
    choices = []
    for r, (m, sel, hit) in enumerate(tops):
        rank = jnp.sum(jnp.where(hit, before, 0.0), axis=-1, keepdims=True).astype(I32)
        choices.append((sel, es[r] / den, rank))
    return _pack_halves(h), choices, jnp.sum(member, axis=0, keepdims=True)


def _route_body(x_ref, g_ref, shp_ref, scp_ref, shs_ref, scs_ref, rw_ref, rb_ref,
                hp_ref, idx_ref, wt_ref, rank_ref, cnt_ref, *, ns):
    @pl.when(pl.program_id(0) == 0)
    def _():
        cnt_ref[...] = jnp.zeros_like(cnt_ref)

    def emit(rows, x, sh, sc):
        hp, choices, tile_cnt = _route_rows(x, g_ref[...], sh, sc, rw_ref[...], rb_ref[...], cnt_ref[...])
        cnt_ref[...] = cnt_ref[...] + tile_cnt
        hp_ref[:rows] = hp
        for r, (sel, wt, rank) in enumerate(choices):
            idx_ref[:rows, r:r + 1] = sel
            wt_ref[:rows, r:r + 1] = wt
            rank_ref[:rows, r:r + 1] = rank

    def prompt():
        emit(x_ref.shape[0], x_ref[...], shp_ref[...], scp_ref[...])

    def sample():
        emit(ns, x_ref[:ns], shs_ref[...], scs_ref[...])
        hp_ref[ns:] = jnp.zeros_like(hp_ref[ns:])
        idx_ref[ns:] = jnp.zeros_like(idx_ref[ns:])
        wt_ref[ns:] = jnp.zeros_like(wt_ref[ns:])
        rank_ref[ns:] = jnp.zeros_like(rank_ref[ns:])

    _prompt_or_sample(pl.program_id(0), pl.num_programs(0), prompt, sample)


def _route(x_all, g, mods, layer, router_w, router_b, np_rows, ns):
    nt = x_all.shape[0]
    tr = 512
    lanes = 128
    rw = jnp.zeros((D, lanes), F32).at[:, :N_EXP].set(router_w[layer])
    rb = jnp.full((1, lanes), -jnp.inf, F32).at[0, :N_EXP].set(router_b[layer])
    shp, shs = mods.specs(layer, 0, tr)
    scp, scs = mods.specs(layer, 1, tr)
    full = lambda shp_: pl.BlockSpec(shp_, lambda i: (0,) * len(shp_))
    rows = lambda w: pl.BlockSpec((tr, w), lambda i: (i, 0))
    return pl.pallas_call(
        functools.partial(_route_body, ns=ns),
        grid=(np_rows // tr + 1,),
        in_specs=[rows(D), pl.BlockSpec((None, 1, D), lambda i: (layer, 0, 0)),
                  shp, scp, shs, scs, full((D, lanes)), full((1, lanes))],
        out_specs=[rows(D // 2), rows(TOPK), rows(TOPK), rows(TOPK), full((1, lanes))],
        out_shape=[jax.ShapeDtypeStruct((nt, D // 2), U32), jax.ShapeDtypeStruct((nt, TOPK), I32),
                   jax.ShapeDtypeStruct((nt, TOPK), F32), jax.ShapeDtypeStruct((nt, TOPK), I32),
                   jax.ShapeDtypeStruct((1, lanes), F32)],
        compiler_params=_cp(1), name="norm2_route",
    )(x_all, g, mods.m4, mods.m4, mods.m3, mods.m3, rw, rb)


def _route_chunk(dest_hbm, dsm, sems, step, n_steps):
    def fetch(s, slot):
        return pltpu.make_async_copy(dest_hbm.at[pl.ds(s * ROUTE_CHUNK, ROUTE_CHUNK)],
                                     dsm.at[pl.ds(slot * ROUTE_CHUNK, ROUTE_CHUNK)], sems.at[slot])

    slot = step % 2

    @pl.when(step == 0)
    def _():
        fetch(0, 0).start()

    fetch(step, slot).wait()

    @pl.when(step + 1 < n_steps)
    def _():
        fetch(step + 1, 1 - slot).start()

    return slot * ROUTE_CHUNK


def _dispatch_body(last_ref, nused_ref, dest_hbm, hp_ref, xs_hbm, dsm, zeros_ref, sem_s, sem_z, sem_r,
                   *, nt, ntiles):
    i = pl.program_id(0)

    def zero_tile(t):
        return pltpu.make_async_copy(zeros_ref, xs_hbm.at[pl.ds(t * MOE_TM, MOE_TM), :], sem_z)

    @pl.when(i == 0)
    def _():
        zeros_ref[...] = jnp.zeros_like(zeros_ref)
        for e in range(N_EXP):
            @pl.when(last_ref[e] >= 0)
            def _():
                zero_tile(last_ref[e]).start()

        def fill(t, c):
            zero_tile(t).start()
            return c

        lax.fori_loop(nused_ref[0], ntiles, fill, 0)
        for e in range(N_EXP):
            @pl.when(last_ref[e] >= 0)
            def _():
                zero_tile(0).wait()

        def fill_wait(t, c):
            zero_tile(0).wait()
            return c

        lax.fori_loop(nused_ref[0], ntiles, fill_wait, 0)

    slot = _route_chunk(dest_hbm, dsm, sem_s, i, pl.num_programs(0))
    rows = jnp.minimum(TOK_CHUNK, nt - i * TOK_CHUNK)

    def row_copy(r, d):
        return pltpu.make_async_copy(hp_ref.at[pl.ds(r, 1), :], xs_hbm.at[pl.ds(d, 1), :], sem_r)

    def issue(r, c):
        for k in range(TOPK):
            row_copy(r, dsm[slot + r * TOPK + k]).start()
        return c

    lax.fori_loop(0, rows, issue, 0)

    def drain(r, c):
        for k in range(TOPK):
            row_copy(0, 0).wait()
        return c

    lax.fori_loop(0, rows, drain, 0)


def _dispatch(hp, dest_pad, last_tile, nused, ntiles):
    nt = hp.shape[0]
    steps = dest_pad.shape[0] // ROUTE_CHUNK
    return pl.pallas_call(
        functools.partial(_dispatch_body, nt=nt, ntiles=ntiles),
        grid_spec=pltpu.PrefetchScalarGridSpec(
            num_scalar_prefetch=2,
            grid=(steps,),
            in_specs=[pl.BlockSpec(memory_space=pl.ANY),
                      pl.BlockSpec((TOK_CHUNK, D // 2), lambda i, last, nu: (i, 0))],
            out_specs=pl.BlockSpec(memory_space=pl.ANY),
            scratch_shapes=[pltpu.SMEM((2 * ROUTE_CHUNK,), I32),
                            pltpu.VMEM((MOE_TM, D // 2), U32),
                            pltpu.SemaphoreType.DMA((2,)), pltpu.SemaphoreType.DMA(()),
                            pltpu.SemaphoreType.DMA(())]),
        out_shape=jax.ShapeDtypeStruct((ntiles * MOE_TM, D // 2), U32),
        compiler_params=_cp(1), name="moe_dispatch",
    )(last_tile, nused, dest_pad, hp)


def _expert_changed(texp_ref, t):
    prev = texp_ref[jnp.maximum(t - 1, 0)]
    return (t == 0) | (texp_ref[t] != prev)


def _by_valid_rows(valid, out_ref, compute_rows):
    half = MOE_TM // 2

    @pl.when(valid > half)
    def _():
        compute_rows(MOE_TM)

    @pl.when((valid > 0) & (valid <= half))
    def _():
        compute_rows(half)
        out_ref[half:] = jnp.zeros_like(out_ref[half:])

    @pl.when(valid == 0)
    def _():
        out_ref[...] = jnp.zeros_like(out_ref)


def _gate_up_body(texp_ref, nused_ref, tvalid_ref, x_ref, wg_ref, wu_ref, bg_ref, bu_ref, act_ref, wg_bf, wu_bf):
    del nused_ref
    t = pl.program_id(1)
    valid = tvalid_ref[t]

    @pl.when((valid > 0) & _expert_changed(texp_ref, t))
    def _():
        wg_bf[...] = wg_ref[...].astype(BF16)
        wu_bf[...] = wu_ref[...].astype(BF16)

    def swiglu_rows(rows):
        x = _unpack_halves(x_ref[:rows])
        gate = jnp.dot(x, wg_bf[...], preferred_element_type=F32) + bg_ref[...]
        up = jnp.dot(x, wu_bf[...], preferred_element_type=F32) + bu_ref[...]
        gate = jnp.minimum(gate, SW_LIMIT)
        up = jnp.clip(up, -SW_LIMIT, SW_LIMIT)
        act_ref[:rows] = ((up + 1.0) * gate * jax.nn.sigmoid(SW_ALPHA * gate)).astype(act_ref.dtype)

    _by_valid_rows(valid, act_ref, swiglu_rows)


def _gate_up(xs, texp, nused, tvalid, w_gate_up, b_gate_up, layer):
    p_rows = xs.shape[0]
    ntiles = p_rows // MOE_TM
    tf = 1024
    nf = DFF // tf
    tile = lambda t, nu: jnp.minimum(t, nu[0] - 1)
    bgu = b_gate_up.reshape(N_LAYERS, N_EXP, 1, 2 * DFF)
    wspec = lambda off: pl.BlockSpec((None, None, D, tf),
                                     lambda f, t, te, nu, tv: (layer, te[tile(t, nu)], 0, off + f))
    bspec = lambda off: pl.BlockSpec((None, None, 1, tf),
                                     lambda f, t, te, nu, tv: (layer, te[tile(t, nu)], 0, off + f))
    return pl.pallas_call(
        _gate_up_body,
        grid_spec=pltpu.PrefetchScalarGridSpec(
            num_scalar_prefetch=3,
            grid=(nf, ntiles),
            in_specs=[pl.BlockSpec((MOE_TM, D // 2), lambda f, t, te, nu, tv: (tile(t, nu), 0)),
                      wspec(0), wspec(nf), bspec(0), bspec(nf)],
            out_specs=pl.BlockSpec((MOE_TM, tf), lambda f, t, te, nu, tv: (t, f)),
            scratch_shapes=[pltpu.VMEM((D, tf), BF16), pltpu.VMEM((D, tf), BF16)]),
        out_shape=jax.ShapeDtypeStruct((p_rows, DFF), BF16),
        compiler_params=_cp(2), name="moe_gate_up",
    )(texp, nused, tvalid, xs, w_gate_up, w_gate_up, bgu, bgu)


def _down_body(texp_ref, nused_ref, tvalid_ref, a_ref, w_ref, b_ref, y_ref, w_bf):
    del nused_ref
    t = pl.program_id(1)
    valid = tvalid_ref[t]

    @pl.when((valid > 0) & _expert_changed(texp_ref, t))
    def _():
        w_bf[...] = w_ref[...].astype(BF16)

    def down_rows(rows):
        y_ref[:rows] = jnp.dot(a_ref[:rows], w_bf[...], preferred_element_type=F32) + b_ref[...]

    _by_valid_rows(valid, y_ref, down_rows)


def _down(act, texp, nused, tvalid, w_down, b_down, layer):
    p_rows = act.shape[0]
    ntiles = p_rows // MOE_TM
    tn = DOWN_TN
    tile = lambda t, nu: jnp.minimum(t, nu[0] - 1)
    bd = b_down.reshape(N_LAYERS, N_EXP, 1, D)
    return pl.pallas_call(
        _down_body,
        grid_spec=pltpu.PrefetchScalarGridSpec(
            num_scalar_prefetch=3,
            grid=(D // tn, ntiles),
            in_specs=[pl.BlockSpec((MOE_TM, DFF), lambda n, t, te, nu, tv: (tile(t, nu), 0)),
                      pl.BlockSpec((None, None, DFF, tn),
                                   lambda n, t, te, nu, tv: (layer, te[tile(t, nu)], 0, n)),
                      pl.BlockSpec((None, None, 1, tn),
                                   lambda n, t, te, nu, tv: (layer, te[tile(t, nu)], 0, n))],
            out_specs=pl.BlockSpec((MOE_TM, tn), lambda n, t, te, nu, tv: (t, n)),
            scratch_shapes=[pltpu.VMEM((DFF, tn), BF16)]),
        out_shape=jax.ShapeDtypeStruct((p_rows, D), F32),
        compiler_params=_cp(2), name="moe_down",
    )(texp, nused, tvalid, act, w_down, bd)


def _combine_body(dest_hbm, ys_hbm, x_ref, wt_ref, gp_ref, gs_ref, *rest, ns, last):
    if last:
        g_ref, yp_ref, ysm_ref, dsm, buf, sem_s, sem_r = rest
    else:
        g_ref, shp_ref, scp_ref, shs_ref, scs_ref, o_ref, h_ref, dsm, buf, sem_s, sem_r = rest
    i = pl.program_id(0)
    slot = _route_chunk(dest_hbm, dsm, sem_s, i, pl.num_programs(0))

    def row_copy(r, k, d):
        return pltpu.make_async_copy(ys_hbm.at[pl.ds(d, 1), :], buf.at[k, pl.ds(r, 1), :], sem_r)

    def combined(rows, gate_ref):
        def issue(r, c):
            for k in range(TOPK):
                row_copy(r, k, dsm[slot + r * TOPK + k]).start()
            return c

        lax.fori_loop(0, rows, issue, 0)

        def drain(r, c):
            for k in range(TOPK):
                row_copy(0, 0, 0).wait()
            return c

        lax.fori_loop(0, rows, drain, 0)
        wt = wt_ref[:rows]
        moe = buf[0, :rows] * wt[:, 0:1]
        for k in range(1, TOPK):
            moe = moe + buf[k, :rows] * wt[:, k:k + 1]
        return x_ref[:rows] + gate_ref[...] * moe

    def prompt():
        x_new = combined(x_ref.shape[0], gp_ref)
        if last:
            yp_ref[...] = _normed(x_new, g_ref[...])
        else:
            o_ref[...] = x_new
            h_ref[...] = (_normed(x_new, g_ref[...]) * (1.0 + scp_ref[...]) + shp_ref[...]).astype(h_ref.dtype)

    def sample():
        x_new = combined(ns, gs_ref)
        if last:
            ysm_ref[...] = _normed(x_new, g_ref[...])
        else:
            o_ref[:ns] = x_new
            o_ref[ns:] = jnp.zeros_like(o_ref[ns:])
            h_ref[:ns] = (_normed(x_new, g_ref[...]) * (1.0 + scs_ref[...]) + shs_ref[...]).astype(h_ref.dtype)
            h_ref[ns:] = jnp.zeros_like(h_ref[ns:])

    _prompt_or_sample(i, pl.num_programs(0), prompt, sample)


def _combine(x_all, ys, dest_pad, wts, mods, layer, np_rows, ns, next_norm):
    nt = x_all.shape[0]
    assert np_rows % TOK_CHUNK == 0 and ns <= TOK_CHUNK
    npt = np_rows // TOK_CHUNK
    last = next_norm[1] is None
    any_spec = pl.BlockSpec(memory_space=pl.ANY)
    gp, gs = mods.specs(layer, 2, TOK_CHUNK)
    rows = lambda w: pl.BlockSpec((TOK_CHUNK, w), lambda i: (i, 0))
    in_specs = [any_spec, any_spec, rows(D), rows(TOPK), gp, gs]
    args = [dest_pad, ys, x_all, wts, *mods.args]
    if last:
        in_specs.append(pl.BlockSpec((1, D), lambda i: (0, 0)))
        args.append(next_norm[0].reshape(1, D))
        out_specs = [pl.BlockSpec((TOK_CHUNK, D), lambda i: (jnp.minimum(i, npt - 1), 0)),
                     pl.BlockSpec((ns, D), lambda i: (0, 0))]
        out_shape = [jax.ShapeDtypeStruct((np_rows, D), F32), jax.ShapeDtypeStruct((ns, D), F32)]
    else:
        g1, mods1 = next_norm
        shp, shs = mods1.specs(layer + 1, 0, TOK_CHUNK)
        scp, scs = mods1.specs(layer + 1, 1, TOK_CHUNK)
        in_specs += [pl.BlockSpec((None, 1, D), lambda i: (layer + 1, 0, 0)), shp, scp, shs, scs]
        args += [g1, mods1.m4, mods1.m4, mods1.m3, mods1.m3]
        out_specs = [rows(D), rows(D)]
        out_shape = [jax.ShapeDtypeStruct((nt, D), F32), jax.ShapeDtypeStruct((nt, D), BF16)]
    return pl.pallas_call(
        functools.partial(_combine_body, ns=ns, last=last),
        grid=(npt + 1,),
        in_specs=in_specs,
        out_specs=out_specs,
        out_shape=out_shape,
        scratch_shapes=[pltpu.SMEM((2 * ROUTE_CHUNK,), I32), pltpu.VMEM((TOPK, TOK_CHUNK, D), F32),
                        pltpu.SemaphoreType.DMA((2,)), pltpu.SemaphoreType.DMA(())],
        compiler_params=_cp(1), name="moe_combine",
    )(*args)


def _moe(x_all, g, mods, layer, router_w, router_b, w_gate_up, b_gate_up, w_down, b_down, np_rows, ns,
         next_norm):
    nt = x_all.shape[0]
    hp, idx, wts, rank, cnt = _route(x_all, g, mods, layer, router_w, router_b, np_rows, ns)
    counts = cnt[0, :N_EXP].astype(I32)
    padded = ((counts + MOE_TM - 1) // MOE_TM) * MOE_TM
    ends = jnp.cumsum(padded)
    gstart = ends - padded
    dest = gstart[idx] + rank
    n_entries = nt * TOPK
    n_chunks = -(-n_entries // ROUTE_CHUNK)
    dest_pad = jnp.zeros((n_chunks * ROUTE_CHUNK,), I32).at[:n_entries].set(dest.reshape(-1))
    ntiles = -(-n_entries // MOE_TM) + N_EXP
    tile_ends = ends // MOE_TM
    tile_ids = jnp.arange(ntiles, dtype=I32)
    texp = jnp.minimum(jnp.sum((tile_ids[:, None] >= tile_ends[None, :]).astype(I32), axis=1), N_EXP - 1)
    nused = tile_ends[-1:].astype(I32)
    last_tile = jnp.where(padded > 0, tile_ends - 1, -1).astype(I32)
    tvalid = jnp.clip((gstart + counts)[texp] - tile_ids * MOE_TM, 0, MOE_TM)
    tvalid = jnp.where(tile_ids < nused[0], tvalid, 0).astype(I32)
    xs = _dispatch(hp, dest_pad, last_tile, nused, ntiles)
    act = _gate_up(xs, texp, nused, tvalid, w_gate_up, b_gate_up, layer)
    ys = _down(act, texp, nused, tvalid, w_down, b_down, layer)
    return _combine(x_all, ys, dest_pad, wts, mods, layer, np_rows, ns, next_norm)


def kernel(x_prompt, x_sample, state_ret, cache_kv_w128_d1, cache_kv_w512_d4, cache_kv_w2048_d16, c_prompt, c_sample, norm1_g, ada1_w, ada1_b, w_in, w_out_ret, w_out_dil, w_out, norm2_g, ada2_w, ada2_b, router_w, router_b, w_gate_up, b_gate_up, w_down, b_down, final_norm_g):
    nb, seq, _ = x_prompt.shape
    ns, dec_seq, _ = x_sample.shape
    assert dec_seq == 1 and seq == 2048 and ns % 16 == 0
    past = cache_kv_w2048_d16.shape[2]
    np_rows = nb * seq
    assert np_rows % ns == 0
    caches = (cache_kv_w128_d1, cache_kv_w512_d4, cache_kv_w2048_d16)

    x_src = (x_prompt.reshape(np_rows, D), x_sample.reshape(ns, D), 0)
    nc = -(-(ns + nb) // 16) * 16
    c_all = jnp.zeros((nc, D), F32).at[:ns].set(c_sample).at[ns:ns + nb].set(c_prompt)
    mods1 = _Mods(_mods(c_all, ada1_w, ada1_b), ns, nb, seq)
    mods2 = _Mods(_mods(c_all, ada2_w, ada2_b), ns, nb, seq)
    g1 = norm1_g.reshape(N_LAYERS, 1, D)
    g2 = norm2_g.reshape(N_LAYERS, 1, D)

    ret_p, ret_s, projs, kv_s = [], [], [], [[] for _ in DIL]
    h = _norm_mod(x_src, g1, mods1, 0, np_rows, ns)
    for layer in range(N_LAYERS):
        proj = _proj(h, w_in, layer)
        projs.append(proj)
        ya_p, st_p = _ret_prompt(proj, nb, seq)
        ya_s, st_s = _ret_sample(proj, state_ret, layer, np_rows, ns, past)
        yb_p = _dil_prompt(proj, nb, seq)
        yb_s = _dil_sample(proj, caches, layer, np_rows, ns)
        x_all = _mix_out(x_src, ya_p, yb_p, ya_s, yb_s, proj, w_out_ret, w_out_dil, w_out, mods1, layer)
        last = layer == N_LAYERS - 1
        x_all, h = _moe(x_all, g2, mods2, layer, router_w, router_b, w_gate_up, b_gate_up, w_down, b_down,
                        np_rows, ns, (final_norm_g, None) if last else (g1, mods1))
        x_src = (x_all, x_all, np_rows // ns)
        ret_p.append(st_p)
        ret_s.append(st_s)
        for g in range(len(DIL)):
            c0 = OFF_DIL + g * 3 * DIL_W + DIL_W
            kv_s[g].append(proj[np_rows:, c0:c0 + 2 * DIL_W].reshape(ns, 1, 2, DIL_H, DIL_E))
    y_p, y_s = x_all, h
    kv_p = [_kv_out(projs, g, nb, seq) for g in range(len(DIL))]
    return (y_p.reshape(nb, seq, D), y_s.reshape(ns, 1, D),
            jnp.stack(ret_p), kv_p[0], kv_p[1], kv_p[2],
            jnp.stack(ret_s), jnp.stack(kv_s[0]), jnp.stack(kv_s[1]), jnp.stack(kv_s[2]))
```

```python
import functools

import jax
import jax.numpy as jnp
from jax import lax
from jax.experimental import pallas as pl
from jax.experimental.pallas import tpu as pltpu

F32 = jnp.float32
BF16 = jnp.bfloat16
U32 = jnp.uint32
I32 = jnp.int32

D = 2048
N_LAYERS = 2
RET_H = 8
RET_DK = 128
RET_DV = 256
RET_C = 128
ROPE_BASE = 10000.0
DIL = ((128, 1), (512, 4), (2048, 16))
DIL_H = 4
DIL_E = 128
DIL_W = DIL_H * DIL_E
N_EXP = 32
TOPK = 4
DFF = 2048
SW_LIMIT = 7.0
SW_ALPHA = 1.702
EPS = 1e-6

OFF_Q = 0
OFF_K = 1024
OFF_V = 2048
OFF_G = 4096
OFF_DIL = 6144
OFF_GA = 10752
OFF_GB = 12800
IN_W = 14848

VMEM_LIMIT = 56 * 1024 * 1024
MOE_TM = 512
DOWN_TN = 2048
ROUTE_CHUNK = 2048
TOK_CHUNK = ROUTE_CHUNK // TOPK


def _cp(n_grid_dims):
    return pltpu.CompilerParams(
        dimension_semantics=("arbitrary",) * n_grid_dims,
        vmem_limit_bytes=VMEM_LIMIT)


def _row_tile(n, cap):
    for k in range(1, n + 1):
        if n % k == 0 and n // k <= cap and (n // k) % 16 == 0:
            return n // k
    raise ValueError((n, cap))


def _silu(x):
    return x * jax.nn.sigmoid(x)


def _mods_body(c_ref, w_ref, b_ref, o_ref):
    a = _silu(c_ref[...]).astype(BF16)
    o_ref[...] = jnp.dot(a, w_ref[...].astype(BF16), preferred_element_type=F32) + b_ref[...]


def _mods(c_all, ada_w, ada_b):
    nc = c_all.shape[0]
    tn = 1024
    return pl.pallas_call(
        _mods_body,
        grid=(N_LAYERS, 3 * D // tn),
        in_specs=[pl.BlockSpec((nc, D), lambda l, n: (0, 0)),
                  pl.BlockSpec((None, D, tn), lambda l, n: (l, 0, n)),
                  pl.BlockSpec((None, 1, tn), lambda l, n: (l, 0, n))],
        out_specs=pl.BlockSpec((None, nc, tn), lambda l, n: (l, 0, n)),
        out_shape=jax.ShapeDtypeStruct((N_LAYERS, nc, 3 * D), F32),
        compiler_params=_cp(2), name="ada_mods",
    )(c_all, ada_w, ada_b.reshape(N_LAYERS, 1, 3 * D))


class _Mods:
    def __init__(self, m, ns, nb, seq):
        self.m3 = m
        self.m4 = m.reshape(m.shape[0], m.shape[1], 1, 3 * D)
        self.ns, self.nb, self.seq = ns, nb, seq

    @property
    def args(self):
        return (self.m4, self.m3)

    def specs(self, layer, which, tr, tn=D, col_axis=False):
        per = self.seq // tr
        ns, nb = self.ns, self.nb
        nblk = D // tn
        if col_axis:
            p_map = lambda n, i: (layer, ns + jnp.minimum(i // per, nb - 1), 0, which * nblk + n)
            s_map = lambda n, i: (layer, 0, which * nblk + n)
        else:
            p_map = lambda i: (layer, ns + jnp.minimum(i // per, nb - 1), 0, which * nblk)
            s_map = lambda i: (layer, 0, which * nblk)
        return [pl.BlockSpec((None, None, 1, tn), p_map), pl.BlockSpec((None, ns, tn), s_map)]


def _prompt_or_sample(step, n_steps, prompt_fn, sample_fn):
    pl.when(step < n_steps - 1)(prompt_fn)
    pl.when(step == n_steps - 1)(sample_fn)


def _normed(x, g):
    return x * lax.rsqrt(jnp.mean(x * x, axis=-1, keepdims=True) + EPS) * g


def _norm_mod_body(xp_ref, xs_ref, g_ref, shp_ref, scp_ref, shs_ref, scs_ref, o_ref, *, ns):
    def prompt():
        y = _normed(xp_ref[...], g_ref[...])
        o_ref[...] = (y * (1.0 + scp_ref[...]) + shp_ref[...]).astype(o_ref.dtype)

    def sample():
        y = _normed(xs_ref[...], g_ref[...])
        o_ref[:ns] = (y * (1.0 + scs_ref[...]) + shs_ref[...]).astype(o_ref.dtype)
        o_ref[ns:] = jnp.zeros_like(o_ref[ns:])

    _prompt_or_sample(pl.program_id(0), pl.num_programs(0), prompt, sample)


def _norm_mod(x_src, g, mods, layer, np_rows, ns):
    x_main, x_samp, sblk = x_src
    tr = 1024
    npt = np_rows // tr
    shp, shs = mods.specs(layer, 0, tr)
    scp, scs = mods.specs(layer, 1, tr)
    return pl.pallas_call(
        functools.partial(_norm_mod_body, ns=ns),
        grid=(npt + 1,),
        in_specs=[pl.BlockSpec((tr, D), lambda i: (jnp.minimum(i, npt - 1), 0)),
                  pl.BlockSpec((ns, D), lambda i: (sblk, 0)),
                  pl.BlockSpec((None, 1, D), lambda i: (layer, 0, 0)),
                  shp, scp, shs, scs],
        out_specs=pl.BlockSpec((tr, D), lambda i: (i, 0)),
        out_shape=jax.ShapeDtypeStruct((np_rows + ns, D), BF16),
        compiler_params=_cp(1), name="norm1",
    )(x_main, x_samp, g, mods.m4, mods.m4, mods.m3, mods.m3)


def _proj_body(x_ref, w_ref, o_ref, wbf_ref):
    @pl.when(pl.program_id(1) == 0)
    def _():
        wbf_ref[...] = w_ref[...].astype(BF16)

    o_ref[...] = jnp.dot(x_ref[...], wbf_ref[...], preferred_element_type=F32).astype(o_ref.dtype)


def _proj(h, w_in, layer):
    nt = h.shape[0]
    tm = _row_tile(nt, 1376)
    tn = 1024
    return pl.pallas_call(
        _proj_body,
        grid=(pl.cdiv(IN_W, tn), nt // tm),
        in_specs=[pl.BlockSpec((tm, D), lambda n, m: (m, 0)),
                  pl.BlockSpec((None, D, tn), lambda n, m: (layer, 0, n))],
        out_specs=pl.BlockSpec((tm, tn), lambda n, m: (m, n)),
        out_shape=jax.ShapeDtypeStruct((nt, IN_W), F32),
        scratch_shapes=[pltpu.VMEM((D, tn), BF16)],
        compiler_params=_cp(2), name="in_proj",
    )(h, w_in)


def _rope_tables(pos):
    half = RET_DK // 2
    inv = ROPE_BASE ** -jnp.linspace(0.0, 1.0, half, dtype=F32)
    ang = pos.astype(F32)[:, None] * inv[None, :]
    cos = jnp.cos(ang)
    sin = jnp.sin(ang)
    return jnp.concatenate([cos, cos], axis=-1), jnp.concatenate([-sin, sin], axis=-1)


def _decay_tables(chunk):
    log_g = jnp.log1p(-jnp.exp2(-5.0 - jnp.arange(RET_H, dtype=F32)))
    idx = jnp.arange(chunk, dtype=F32)
    rel = idx[:, None] - idx[None, :]
    inner = jnp.where(rel >= 0, jnp.exp(log_g[:, None, None] * jnp.maximum(rel, 0.0)), 0.0)
    q_decay = jnp.exp(log_g[:, None] * (idx + 1.0)[None, :])
    k_decay = jnp.exp(log_g[:, None] * (chunk - 1.0 - idx)[None, :])
    chunk_decay = jnp.exp(log_g * chunk)
    return inner, q_decay, k_decay, chunk_decay


def _rot(x, cos, sin):
    return x * cos + pltpu.roll(x, RET_DK // 2, 1) * sin


def _head_norm_gate(o, g):
    on = o * lax.rsqrt(jnp.mean(o * o, axis=-1, keepdims=True) + EPS)
    return _silu(g) * on


def _ret_prompt_body(q_ref, k_ref, v_ref, g_ref, cos_ref, sin_ref, inner_ref, qd_ref, kd_ref, cd_ref,
                     ya_ref, s_ref):
    @pl.when(pl.program_id(1) == 0)
    def _():
        s_ref[...] = jnp.zeros_like(s_ref)

    for c in range(RET_STEP // RET_C):
        rows = slice(c * RET_C, (c + 1) * RET_C)
        cos = cos_ref[rows, :]
        sin = sin_ref[rows, :]
        for h in range(RET_H):
            q = _rot(q_ref[rows, h * RET_DK:(h + 1) * RET_DK], cos, sin)
            k = _rot(k_ref[rows, h * RET_DK:(h + 1) * RET_DK], cos, sin) * (RET_DK ** -0.5)
            vb = v_ref[rows, h * RET_DV:(h + 1) * RET_DV].astype(BF16)
            s = lax.dot_general(q.astype(BF16), k.astype(BF16), (((1,), (1,)), ((), ())),
                                preferred_element_type=F32) * inner_ref[h]
            state = s_ref[h]
            o = (jnp.dot(s.astype(BF16), vb, preferred_element_type=F32)
                 + jnp.dot((q * qd_ref[h]).astype(BF16), state.astype(BF16), preferred_element_type=F32))
            kt = (k * kd_ref[h]).T.astype(BF16)
            s_ref[h] = state * cd_ref[h] + jnp.dot(kt, vb, preferred_element_type=F32)
            y = _head_norm_gate(o, g_ref[rows, h * RET_DV:(h + 1) * RET_DV])
            ya_ref[rows, h * RET_DV:(h + 1) * RET_DV] = y.astype(ya_ref.dtype)


RET_STEP = 4 * RET_C


def _ret_prompt(proj, nb, seq):
    nch = seq // RET_STEP
    cos, sin = _rope_tables(jnp.arange(seq, dtype=I32))
    inner, qd, kd, cd = _decay_tables(RET_C)
    qd_b = jnp.broadcast_to(qd[:, :, None], (RET_H, RET_C, RET_DK))
    kd_b = jnp.broadcast_to(kd[:, :, None], (RET_H, RET_C, RET_DK))
    cd_b = jnp.broadcast_to(cd[:, None, None], (RET_H, 1, RET_DV))
    row = lambda b, c: b * nch + c
    full3 = lambda shp: pl.BlockSpec(shp, lambda b, c: (0, 0, 0))
    return pl.pallas_call(
        _ret_prompt_body,
        grid=(nb, nch),
        in_specs=[pl.BlockSpec((RET_STEP, 1024), lambda b, c: (row(b, c), OFF_Q // 1024)),
                  pl.BlockSpec((RET_STEP, 1024), lambda b, c: (row(b, c), OFF_K // 1024)),
                  pl.BlockSpec((RET_STEP, 2048), lambda b, c: (row(b, c), OFF_V // 2048)),
                  pl.BlockSpec((RET_STEP, 2048), lambda b, c: (row(b, c), OFF_G // 2048)),
                  pl.BlockSpec((RET_STEP, RET_DK), lambda b, c: (c, 0)),
                  pl.BlockSpec((RET_STEP, RET_DK), lambda b, c: (c, 0)),
                  full3((RET_H, RET_C, RET_C)), full3((RET_H, RET_C, RET_DK)),
                  full3((RET_H, RET_C, RET_DK)), full3((RET_H, 1, RET_DV))],
        out_specs=[pl.BlockSpec((RET_STEP, D), lambda b, c: (row(b, c), 0)),
                   pl.BlockSpec((None, RET_H, RET_DK, RET_DV), lambda b, c: (b, 0, 0, 0))],
        out_shape=[jax.ShapeDtypeStruct((nb * seq, D), BF16),
                   jax.ShapeDtypeStruct((nb, RET_H, RET_DK, RET_DV), F32)],
        compiler_params=_cp(2), name="retention_prompt",
    )(proj, proj, proj, proj, cos, sin, inner, qd_b, kd_b, cd_b)


RS_NB = 8


def _ret_sample_body(q_ref, k_ref, v_ref, g_ref, cos_ref, sin_ref, inner_ref, qd_ref, kd_ref, cd_ref,
                     st_ref, ya_ref, so_ref):
    i = pl.program_id(0)
    cos = cos_ref[...]
    sin = sin_ref[...]
    row_i = lax.broadcasted_iota(I32, (RET_DK, RET_DK), 0)
    col_i = lax.broadcasted_iota(I32, (RET_DK, RET_DK), 1)
    eye = row_i == col_i
    rows = pl.ds(pl.multiple_of(i * RS_NB, RS_NB), RS_NB)
    q_all = q_ref[rows, :]
    k_all = k_ref[rows, :]
    v_all = v_ref[rows, :]
    g_all = g_ref[rows, :]
    out_rows = []
    for j in range(RS_NB):
        heads = []
        for h in range(RET_H):
            q = _rot(q_all[j:j + 1, h * RET_DK:(h + 1) * RET_DK], cos, sin)
            k = _rot(k_all[j:j + 1, h * RET_DK:(h + 1) * RET_DK], cos, sin) * (RET_DK ** -0.5)
            v = v_all[j:j + 1, h * RET_DV:(h + 1) * RET_DV]
            s = jnp.sum(q * k, axis=-1, keepdims=True) * inner_ref[h]
            state = st_ref[j, h]
            q8 = jnp.broadcast_to(q * qd_ref[h], (8, RET_DK)).astype(BF16)
            qs = jnp.dot(q8, state.astype(BF16), preferred_element_type=F32)[0:1]
            o = s * v + qs
            kdiag = jnp.where(eye, jnp.broadcast_to(k * kd_ref[h], (RET_DK, RET_DK)), 0.0).astype(BF16)
            vb = jnp.broadcast_to(v, (RET_DK, RET_DV)).astype(BF16)
            so_ref[j, h] = state * cd_ref[h] + jnp.dot(kdiag, vb, preferred_element_type=F32)
            heads.append(_head_norm_gate(o, g_all[j:j + 1, h * RET_DV:(h + 1) * RET_DV]))
        out_rows.append(jnp.concatenate(heads, axis=1))
    ya_ref[...] = jnp.concatenate(out_rows, axis=0)


def _ret_sample(proj, state_ret, layer, np_rows, ns, past):
    cos, sin = _rope_tables(past + jnp.arange(1, dtype=I32))
    inner, qd, kd, cd = _decay_tables(1)
    inner_b = inner.reshape(RET_H, 1, 1)
    qd_b = jnp.broadcast_to(qd[:, :, None], (RET_H, 1, RET_DK))
    kd_b = jnp.broadcast_to(kd[:, :, None], (RET_H, 1, RET_DK))
    cd_b = jnp.broadcast_to(cd[:, None, None], (RET_H, 1, RET_DV))
    sblk = np_rows // ns
    full = lambda shp: pl.BlockSpec(shp, lambda i: (0,) * len(shp))
    return pl.pallas_call(
        _ret_sample_body,
        grid=(ns // RS_NB,),
        in_specs=[pl.BlockSpec((ns, 1024), lambda i: (sblk, OFF_Q // 1024)),
                  pl.BlockSpec((ns, 1024), lambda i: (sblk, OFF_K // 1024)),
                  pl.BlockSpec((ns, 2048), lambda i: (sblk, OFF_V // 2048)),
                  pl.BlockSpec((ns, 2048), lambda i: (sblk, OFF_G // 2048)),
                  full((1, RET_DK)), full((1, RET_DK)),
                  full((RET_H, 1, 1)), full((RET_H, 1, RET_DK)), full((RET_H, 1, RET_DK)),
                  full((RET_H, 1, RET_DV)),
                  pl.BlockSpec((None, RS_NB, RET_H, RET_DK, RET_DV), lambda i: (layer, i, 0, 0, 0))],
        out_specs=[pl.BlockSpec((RS_NB, D), lambda i: (i, 0)),
                   pl.BlockSpec((RS_NB, RET_H, RET_DK, RET_DV), lambda i: (i, 0, 0, 0))],
        out_shape=[jax.ShapeDtypeStruct((ns, D), F32),
                   jax.ShapeDtypeStruct((ns, RET_H, RET_DK, RET_DV), F32)],
        compiler_params=_cp(1), name="retention_sample",
    )(proj, proj, proj, proj, cos, sin, inner_b, qd_b, kd_b, cd_b, state_ret)


def _merge_groups(parts):
    m_all = parts[0][1]
    for _, m_g, _ in parts[1:]:
        m_all = jnp.maximum(m_all, m_g)
    num = 0.0
    den = 0.0
    for n_g, m_g, l_g in parts:
        w = jnp.exp(m_g - m_all)
        num = num + n_g * w
        den = den + l_g * w
    return num / den


def _dil_prompt_body(*refs, seq):
    qkv = refs[:9]
    yb_ref = refs[9]
    num_ref, m_ref, l_ref = refs[10:13]
    band = 128
    scale = DIL_E ** -0.5
    qi = lax.broadcasted_iota(I32, (band, band), 0)
    kj1 = lax.broadcasted_iota(I32, (band, band), 1)
    qi2 = lax.broadcasted_iota(I32, (band, 2 * band), 0)
    kj2 = lax.broadcasted_iota(I32, (band, 2 * band), 1)
    first_valid = kj1 <= qi
    later_valid = (kj2 >= qi2) & (kj2 <= qi2 + band)
    for g, (window, dil) in enumerate(DIL):
        assert window // dil == band
        q_ref, k_ref, v_ref = qkv[3 * g:3 * g + 3]
        nblk = seq // dil // band
        for p in range(dil):
            for n in range(nblk):
                rows_q = pl.ds(p + n * band * dil, band, stride=dil)
                qb = q_ref[rows_q, :].astype(BF16)
                if n == 0:
                    rows_k = rows_q
                    valid = first_valid
                else:
                    rows_k = pl.ds(p + (n - 1) * band * dil, 2 * band, stride=dil)
                    valid = later_valid
                kk = k_ref[rows_k, :].astype(BF16)
                vv = v_ref[rows_k, :].astype(BF16)
                s = lax.dot_general(qb, kk, (((1,), (1,)), ((), ())), preferred_element_type=F32) * scale
                s = jnp.where(valid, s, -jnp.inf)
                m = jnp.max(s, axis=-1, keepdims=True)
                pe = jnp.exp(s - m)
                l = jnp.sum(pe, axis=-1, keepdims=True)
                num_ref[g, rows_q, :] = jnp.dot(pe.astype(BF16), vv, preferred_element_type=F32)
                m_ref[g, rows_q, :] = jnp.broadcast_to(m, (band, DIL_E))
                l_ref[g, rows_q, :] = jnp.broadcast_to(l, (band, DIL_E))
    parts = [(num_ref[g], m_ref[g], l_ref[g]) for g in range(len(DIL))]
    yb_ref[...] = _merge_groups(parts).astype(yb_ref.dtype)


def _dil_prompt(proj, nb, seq):
    in_specs = []
    for g in range(len(DIL)):
        for j in range(3):
            cb = (OFF_DIL + g * 3 * DIL_W + j * DIL_W) // DIL_E
            in_specs.append(pl.BlockSpec((seq, DIL_E), lambda b, h, cb=cb: (b, cb + h)))
    scr = pltpu.VMEM((len(DIL), seq, DIL_E), F32)
    return pl.pallas_call(
        functools.partial(_dil_prompt_body, seq=seq),
        grid=(nb, DIL_H),
        in_specs=in_specs,
        out_specs=pl.BlockSpec((seq, DIL_E), lambda b, h: (b, h)),
        out_shape=jax.ShapeDtypeStruct((nb * seq, DIL_W), BF16),
        scratch_shapes=[scr, scr, scr],
        compiler_params=_cp(2), name="dilated_prompt",
    )(*([proj] * 9))


DS_NB = 8


def _dil_sample_body(c0_ref, c1_ref, c2_ref, q0_ref, q1_ref, q2_ref, yb_ref):
    i = pl.program_id(0)
    caches = (c0_ref, c1_ref, c2_ref)
    scale = DIL_E ** -0.5
    rows = pl.ds(pl.multiple_of(i * DS_NB, DS_NB), DS_NB)
    news = [ref[rows, :] for ref in (q0_ref, q1_ref, q2_ref)]
    out_rows = []
    for j in range(DS_NB):
        parts = []
        for g in range(len(DIL)):
            new = news[g][j:j + 1]
            q4 = jnp.concatenate([new[:, h * DIL_E:(h + 1) * DIL_E] for h in range(DIL_H)], axis=0)
            k4 = jnp.concatenate([new[:, DIL_W + h * DIL_E:DIL_W + (h + 1) * DIL_E] for h in range(DIL_H)], axis=0)
            v4 = jnp.concatenate([new[:, 2 * DIL_W + h * DIL_E:2 * DIL_W + (h + 1) * DIL_E] for h in range(DIL_H)],
                                 axis=0)
            kc = caches[g][j, :, 0:DIL_H, :]
            vc = caches[g][j, :, DIL_H:2 * DIL_H, :]
            s_c = jnp.sum(kc * q4[None], axis=-1, keepdims=True) * scale
            s_n = jnp.sum(k4 * q4, axis=-1, keepdims=True) * scale
            m = jnp.maximum(jnp.max(s_c, axis=0), s_n)
            p_c = jnp.exp(s_c - m[None])
            p_n = jnp.exp(s_n - m)
            l = jnp.sum(p_c, axis=0) + p_n
            num = jnp.sum(p_c * vc, axis=0) + p_n * v4
            parts.append((num, m, l))
        y = _merge_groups(parts)
        out_rows.append(jnp.concatenate([y[h:h + 1] for h in range(DIL_H)], axis=1))
    yb_ref[...] = jnp.concatenate(out_rows, axis=0)


def _dil_sample(proj, caches, layer, np_rows, ns):
    sblk = np_rows // ns
    views = []
    in_specs = []
    for cache, (window, dil) in zip(caches, DIL):
        wb = cache.shape[2]
        assert wb == window and wb % dil == 0
        views.append(cache.reshape(cache.shape[0] * ns, wb // dil, dil, 2 * DIL_H, DIL_E))
        in_specs.append(pl.BlockSpec((DS_NB, wb // dil, None, 2 * DIL_H, DIL_E),
                                     lambda i: (layer * (ns // DS_NB) + i, 0, 0, 0, 0)))
    for g in range(len(DIL)):
        in_specs.append(pl.BlockSpec((ns, 3 * DIL_W), lambda i, g=g: (sblk, OFF_DIL // (3 * DIL_W) + g)))
    return pl.pallas_call(
        _dil_sample_body,
        grid=(ns // DS_NB,),
        in_specs=in_specs,
        out_specs=pl.BlockSpec((DS_NB, DIL_W), lambda i: (i, 0)),
        out_shape=jax.ShapeDtypeStruct((ns, DIL_W), F32),
        compiler_params=_cp(1), name="dilated_sample",
    )(*views, proj, proj, proj)


def _kv_out_body(*refs):
    o_ref = refs[-1]
    layer = pl.program_id(0)
    for l in range(N_LAYERS):
        @pl.when(layer == l)
        def _(k_ref=refs[2 * l], v_ref=refs[2 * l + 1]):
            for h in range(DIL_H):
                o_ref[:, 0, h, :] = k_ref[:, h * DIL_E:(h + 1) * DIL_E]
                o_ref[:, 1, h, :] = v_ref[:, h * DIL_E:(h + 1) * DIL_E]


def _kv_out(projs, g, nb, seq):
    window = min(DIL[g][0], seq)
    rows = min(window, 512)
    nr = window // rows
    cb = (OFF_DIL + g * 3 * DIL_W + DIL_W) // DIL_W
    first = (seq - window) // rows
    per_b = seq // rows
    last = (nb - 1) * per_b + first + nr - 1
    in_specs = []
    for l in range(N_LAYERS):
        def rblk(ll, b, r, l=l):
            cur = b * per_b + first + r
            return jnp.where(ll == l, cur, jnp.where(ll < l, first, last))
        in_specs.append(pl.BlockSpec((rows, DIL_W), lambda ll, b, r, f=rblk: (f(ll, b, r), cb)))
        in_specs.append(pl.BlockSpec((rows, DIL_W), lambda ll, b, r, f=rblk: (f(ll, b, r), cb + 1)))
    args = []
    for p in projs:
        args += [p, p]
    return pl.pallas_call(
        _kv_out_body,
        grid=(N_LAYERS, nb, nr),
        in_specs=in_specs,
        out_specs=pl.BlockSpec((None, None, rows, 2, DIL_H, DIL_E), lambda ll, b, r: (ll, b, r, 0, 0, 0)),
        out_shape=jax.ShapeDtypeStruct((N_LAYERS, nb, window, 2, DIL_H, DIL_E), F32),
        compiler_params=_cp(3), name="kv_prompt_out",
    )(*args)


W_CHUNK = 256


def _load_weight_bf16(w_hbm, layer, dst_bf, stage, sems):
    n = dst_bf.shape[0] // W_CHUNK

    def chunk(c):
        return pltpu.make_async_copy(w_hbm.at[layer, pl.ds(c * W_CHUNK, W_CHUNK), :], stage.at[c % 2],
                                     sems.at[c % 2])

    chunk(0).start()
    for c in range(n):
        if c + 1 < n:
            chunk(c + 1).start()
        chunk(c).wait()
        dst_bf[c * W_CHUNK:(c + 1) * W_CHUNK, :] = stage[c % 2].astype(BF16)


GATE_TN = 512


def _mix_out_body(xp_ref, xs_ref, yap_ref, ybp_ref, yas_ref, ybs_ref, *rest, ns, layer):
    n_gate = D // GATE_TN
    ga_refs, gb_refs = rest[:n_gate], rest[n_gate:2 * n_gate]
    gp_ref, gs_ref, wr_hbm, wd_hbm, wo_hbm, o_ref, wr_bf, wd_bf, wo_bf, stage, sems = rest[2 * n_gate:]

    @pl.when(pl.program_id(0) == 0)
    def _():
        _load_weight_bf16(wr_hbm, layer, wr_bf, stage, sems)
        _load_weight_bf16(wd_hbm, layer, wd_bf, stage, sems)
        _load_weight_bf16(wo_hbm, layer, wo_bf, stage, sems)

    def mixed(rows, ya, yb, x, gate):
        parts = []
        for c in range(n_gate):
            cols = slice(c * GATE_TN, (c + 1) * GATE_TN)
            y_a = jnp.dot(ya, wr_bf[:, cols], preferred_element_type=F32)
            y_b = jnp.dot(yb, wd_bf[:, cols], preferred_element_type=F32)
            merged = jax.nn.sigmoid(ga_refs[c][:rows]) * y_a + jax.nn.sigmoid(gb_refs[c][:rows]) * y_b
            parts.append(merged.astype(BF16))
        mix = jnp.dot(jnp.concatenate(parts, axis=1), wo_bf[...], preferred_element_type=F32)
        return x + gate * mix

    def prompt():
        o_ref[...] = mixed(o_ref.shape[0], yap_ref[...], ybp_ref[...], xp_ref[...], gp_ref[...])

    def sample():
        o_ref[:ns] = mixed(ns, yas_ref[...].astype(BF16), ybs_ref[...].astype(BF16), xs_ref[...], gs_ref[...])
        o_ref[ns:] = jnp.zeros_like(o_ref[ns:])

    _prompt_or_sample(pl.program_id(0), pl.num_programs(0), prompt, sample)


def _mix_out(x_src, ya_p, yb_p, ya_s, yb_s, proj, w_out_ret, w_out_dil, w_out, mods, layer):
    x_main, x_samp, sblk = x_src
    nt = proj.shape[0]
    np_rows = ya_p.shape[0]
    ns = ya_s.shape[0]
    tm = 256
    npt = np_rows // tm
    ptile = lambda i: jnp.minimum(i, npt - 1)
    gp, gs = mods.specs(layer, 2, tm)
    any_spec = pl.BlockSpec(memory_space=pl.ANY)
    gate_specs = [pl.BlockSpec((tm, GATE_TN), lambda i, cb=(off // GATE_TN + c): (i, cb))
                  for off in (OFF_GA, OFF_GB) for c in range(D // GATE_TN)]
    return pl.pallas_call(
        functools.partial(_mix_out_body, ns=ns, layer=layer),
        grid=(npt + 1,),
        in_specs=[pl.BlockSpec((tm, D), lambda i: (ptile(i), 0)),
                  pl.BlockSpec((ns, D), lambda i: (sblk, 0)),
                  pl.BlockSpec((tm, D), lambda i: (ptile(i), 0)),
                  pl.BlockSpec((tm, DIL_W), lambda i: (ptile(i), 0)),
                  pl.BlockSpec((ns, D), lambda i: (0, 0)),
                  pl.BlockSpec((ns, DIL_W), lambda i: (0, 0)),
                  *gate_specs, gp, gs, any_spec, any_spec, any_spec],
        out_specs=pl.BlockSpec((tm, D), lambda i: (i, 0)),
        out_shape=jax.ShapeDtypeStruct((nt, D), F32),
        scratch_shapes=[pltpu.VMEM((D, D), BF16), pltpu.VMEM((DIL_W, D), BF16), pltpu.VMEM((D, D), BF16),
                        pltpu.VMEM((2, W_CHUNK, D), F32), pltpu.SemaphoreType.DMA((2,))],
        compiler_params=_cp(1), name="mixer_out",
    )(x_main, x_samp, ya_p, yb_p, ya_s, yb_s, *([proj] * (2 * (D // GATE_TN))), *mods.args,
      w_out_ret, w_out_dil, w_out)


def _pack_halves(h):
    a = lax.bitcast_convert_type(h[:, :D // 2].astype(BF16).astype(F32), U32)
    b = lax.bitcast_convert_type(h[:, D // 2:].astype(BF16).astype(F32), U32)
    return a | (b >> 16)


def _unpack_halves(w):
    a = lax.bitcast_convert_type(w & jnp.uint32(0xFFFF0000), F32)
    b = lax.bitcast_convert_type(w << 16, F32)
    return jnp.concatenate([a.astype(BF16), b.astype(BF16)], axis=1)


def _route_rows(x, g, sh, sc, rw, rb, cnt):
    tr = x.shape[0]
    h = _normed(x, g) * (1.0 + sc) + sh
    logits = jnp.dot(h, rw, precision=lax.Precision.HIGHEST, preferred_element_type=F32) + rb
    lane = lax.broadcasted_iota(I32, logits.shape, 1)
    work = logits
    tops = []
    member = jnp.zeros(logits.shape, F32)
    for _ in range(TOPK):
        m = jnp.max(work, axis=-1, keepdims=True)
        sel = jnp.min(jnp.where(work == m, lane, logits.shape[1]), axis=-1, keepdims=True)
        hit = lane == sel
        work = jnp.where(hit, -jnp.inf, work)
        member = jnp.where(hit, 1.0, member)
        tops.append((m, sel, hit))
    ri = lax.broadcasted_iota(I32, (tr, tr), 0)
    ci = lax.broadcasted_iota(I32, (tr, tr), 1)
    tri = jnp.where(ci < ri, 1.0, 0.0).astype(BF16)
    before = jnp.dot(tri, member.astype(BF16), preferred_element_type=F32) + cnt
    es = [jnp.exp(m - tops[0][0]) for m, _, _ in tops]
    den = es[0]
    for e in es[1:]:
        den = den + e
    choices = []
    for r, (m, sel, hit) in enumerate(tops):
        rank = jnp.sum(jnp.where(hit, before, 0.0), axis=-1, keepdims=True).astype(I32)
        choices.append((sel, es[r] / den, rank))
    return _pack_halves(h), choices, jnp.sum(member, axis=0, keepdims=True)


def _route_body(x_ref, g_ref, shp_ref, scp_ref, shs_ref, scs_ref, rw_ref, rb_ref,
                hp_ref, idx_ref, wt_ref, rank_ref, cnt_ref, *, ns):
    @pl.when(pl.program_id(0) == 0)
    def _():
        cnt_ref[...] = jnp.zeros_like(cnt_ref)

    def emit(rows, x, sh, sc):
        hp, choices, tile_cnt = _route_rows(x, g_ref[...], sh, sc, rw_ref[...], rb_ref[...], cnt_ref[...])
        cnt_ref[...] = cnt_ref[...] + tile_cnt
        hp_ref[:rows] = hp
        for r, (sel, wt, rank) in enumerate(choices):
            idx_ref[:rows, r:r + 1] = sel
            wt_ref[:rows, r:r + 1] = wt
            rank_ref[:rows, r:r + 1] = rank

    def prompt():
        emit(x_ref.shape[0], x_ref[...], shp_ref[...], scp_ref[...])

    def sample():
        emit(ns, x_ref[:ns], shs_ref[...], scs_ref[...])
        hp_ref[ns:] = jnp.zeros_like(hp_ref[ns:])
        idx_ref[ns:] = jnp.zeros_like(idx_ref[ns:])
        wt_ref[ns:] = jnp.zeros_like(wt_ref[ns:])
        rank_ref[ns:] = jnp.zeros_like(rank_ref[ns:])

    _prompt_or_sample(pl.program_id(0), pl.num_programs(0), prompt, sample)


def _route(x_all, g, mods, layer, router_w, router_b, np_rows, ns):
    nt = x_all.shape[0]
    tr = 512
    lanes = 128
    rw = jnp.zeros((D, lanes), F32).at[:, :N_EXP].set(router_w[layer])
    rb = jnp.full((1, lanes), -jnp.inf, F32).at[0, :N_EXP].set(router_b[layer])
    shp, shs = mods.specs(layer, 0, tr)
    scp, scs = mods.specs(layer, 1, tr)
    full = lambda shp_: pl.BlockSpec(shp_, lambda i: (0,) * len(shp_))
    rows = lambda w: pl.BlockSpec((tr, w), lambda i: (i, 0))
    return pl.pallas_call(
        functools.partial(_route_body, ns=ns),
        grid=(np_rows // tr + 1,),
        in_specs=[rows(D), pl.BlockSpec((None, 1, D), lambda i: (layer, 0, 0)),
                  shp, scp, shs, scs, full((D, lanes)), full((1, lanes))],
        out_specs=[rows(D // 2), rows(TOPK), rows(TOPK), rows(TOPK), full((1, lanes))],
        out_shape=[jax.ShapeDtypeStruct((nt, D // 2), U32), jax.ShapeDtypeStruct((nt, TOPK), I32),
                   jax.ShapeDtypeStruct((nt, TOPK), F32), jax.ShapeDtypeStruct((nt, TOPK), I32),
                   jax.ShapeDtypeStruct((1, lanes), F32)],
        compiler_params=_cp(1), name="norm2_route",
    )(x_all, g, mods.m4, mods.m4, mods.m3, mods.m3, rw, rb)


def _route_chunk(dest_hbm, dsm, sems, step, n_steps):
    def fetch(s, slot):
        return pltpu.make_async_copy(dest_hbm.at[pl.ds(s * ROUTE_CHUNK, ROUTE_CHUNK)],
                                     dsm.at[pl.ds(slot * ROUTE_CHUNK, ROUTE_CHUNK)], sems.at[slot])

    slot = step % 2

    @pl.when(step == 0)
    def _():
        fetch(0, 0).start()

    fetch(step, slot).wait()

    @pl.when(step + 1 < n_steps)
    def _():
        fetch(step + 1, 1 - slot).start()

    return slot * ROUTE_CHUNK


def _dispatch_body(last_ref, nused_ref, dest_hbm, hp_ref, xs_hbm, dsm, zeros_ref, sem_s, sem_z, sem_r,
                   *, nt, ntiles):
    i = pl.program_id(0)

    def zero_tile(t):
        return pltpu.make_async_copy(zeros_ref, xs_hbm.at[pl.ds(t * MOE_TM, MOE_TM), :], sem_z)

    @pl.when(i == 0)
    def _():
        zeros_ref[...] = jnp.zeros_like(zeros_ref)
        for e in range(N_EXP):
            @pl.when(last_ref[e] >= 0)
            def _():
                zero_tile(last_ref[e]).start()

        def fill(t, c):
            zero_tile(t).start()
            return c

        lax.fori_loop(nused_ref[0], ntiles, fill, 0)
        for e in range(N_EXP):
            @pl.when(last_ref[e] >= 0)
            def _():
                zero_tile(0).wait()

        def fill_wait(t, c):
            zero_tile(0).wait()
            return c

        lax.fori_loop(nused_ref[0], ntiles, fill_wait, 0)

    slot = _route_chunk(dest_hbm, dsm, sem_s, i, pl.num_programs(0))
    rows = jnp.minimum(TOK_CHUNK, nt - i * TOK_CHUNK)

    def row_copy(r, d):
        return pltpu.make_async_copy(hp_ref.at[pl.ds(r, 1), :], xs_hbm.at[pl.ds(d, 1), :], sem_r)

    def issue(r, c):
        for k in range(TOPK):
            row_copy(r, dsm[slot + r * TOPK + k]).start()
        return c

    lax.fori_loop(0, rows, issue, 0)

    def drain(r, c):
        for k in range(TOPK):
            row_copy(0, 0).wait()
        return c

    lax.fori_loop(0, rows, drain, 0)


def _dispatch(hp, dest_pad, last_tile, nused, ntiles):
    nt = hp.shape[0]
    steps = dest_pad.shape[0] // ROUTE_CHUNK
    return pl.pallas_call(
        functools.partial(_dispatch_body, nt=nt, ntiles=ntiles),
        grid_spec=pltpu.PrefetchScalarGridSpec(
            num_scalar_prefetch=2,
            grid=(steps,),
            in_specs=[pl.BlockSpec(memory_space=pl.ANY),
                      pl.BlockSpec((TOK_CHUNK, D // 2), lambda i, last, nu: (i, 0))],
            out_specs=pl.BlockSpec(memory_space=pl.ANY),
            scratch_shapes=[pltpu.SMEM((2 * ROUTE_CHUNK,), I32),
                            pltpu.VMEM((MOE_TM, D // 2), U32),
                            pltpu.SemaphoreType.DMA((2,)), pltpu.SemaphoreType.DMA(()),
                            pltpu.SemaphoreType.DMA(())]),
        out_shape=jax.ShapeDtypeStruct((ntiles * MOE_TM, D // 2), U32),
        compiler_params=_cp(1), name="moe_dispatch",
    )(last_tile, nused, dest_pad, hp)


def _expert_changed(texp_ref, t):
    prev = texp_ref[jnp.maximum(t - 1, 0)]
    return (t == 0) | (texp_ref[t] != prev)


def _by_valid_rows(valid, out_ref, compute_rows):
    half = MOE_TM // 2

    @pl.when(valid > half)
    def _():
        compute_rows(MOE_TM)

    @pl.when((valid > 0) & (valid <= half))
    def _():
        compute_rows(half)
        out_ref[half:] = jnp.zeros_like(out_ref[half:])

    @pl.when(valid == 0)
    def _():
        out_ref[...] = jnp.zeros_like(out_ref)


def _gate_up_body(texp_ref, nused_ref, tvalid_ref, x_ref, wg_ref, wu_ref, bg_ref, bu_ref, act_ref, wg_bf, wu_bf):
    del nused_ref
    t = pl.program_id(1)
    valid = tvalid_ref[t]

    @pl.when((valid > 0) & _expert_changed(texp_ref, t))
    def _():
        wg_bf[...] = wg_ref[...].astype(BF16)
        wu_bf[...] = wu_ref[...].astype(BF16)

    def swiglu_rows(rows):
        x = _unpack_halves(x_ref[:rows])
        gate = jnp.dot(x, wg_bf[...], preferred_element_type=F32) + bg_ref[...]
        up = jnp.dot(x, wu_bf[...], preferred_element_type=F32) + bu_ref[...]
        gate = jnp.minimum(gate, SW_LIMIT)
        up = jnp.clip(up, -SW_LIMIT, SW_LIMIT)
        act_ref[:rows] = ((up + 1.0) * gate * jax.nn.sigmoid(SW_ALPHA * gate)).astype(act_ref.dtype)

    _by_valid_rows(valid, act_ref, swiglu_rows)


def _gate_up(xs, texp, nused, tvalid, w_gate_up, b_gate_up, layer):
    p_rows = xs.shape[0]
    ntiles = p_rows // MOE_TM
    tf = 1024
    nf = DFF // tf
    tile = lambda t, nu: jnp.minimum(t, nu[0] - 1)
    bgu = b_gate_up.reshape(N_LAYERS, N_EXP, 1, 2 * DFF)
    wspec = lambda off: pl.BlockSpec((None, None, D, tf),
                                     lambda f, t, te, nu, tv: (layer, te[tile(t, nu)], 0, off + f))
    bspec = lambda off: pl.BlockSpec((None, None, 1, tf),
                                     lambda f, t, te, nu, tv: (layer, te[tile(t, nu)], 0, off + f))
    return pl.pallas_call(
        _gate_up_body,
        grid_spec=pltpu.PrefetchScalarGridSpec(
            num_scalar_prefetch=3,
            grid=(nf, ntiles),
            in_specs=[pl.BlockSpec((MOE_TM, D // 2), lambda f, t, te, nu, tv: (tile(t, nu), 0)),
                      wspec(0), wspec(nf), bspec(0), bspec(nf)],
            out_specs=pl.BlockSpec((MOE_TM, tf), lambda f, t, te, nu, tv: (t, f)),
            scratch_shapes=[pltpu.VMEM((D, tf), BF16), pltpu.VMEM((D, tf), BF16)]),
        out_shape=jax.ShapeDtypeStruct((p_rows, DFF), BF16),
        compiler_params=_cp(2), name="moe_gate_up",
    )(texp, nused, tvalid, xs, w_gate_up, w_gate_up, bgu, bgu)


def _down_body(texp_ref, nused_ref, tvalid_ref, a_ref, w_ref, b_ref, y_ref, w_bf):
    del nused_ref
    t = pl.program_id(1)
    valid = tvalid_ref[t]

    @pl.when((valid > 0) & _expert_changed(texp_ref, t))
    def _():
        w_bf[...] = w_ref[...].astype(BF16)

    def down_rows(rows):
        y_ref[:rows] = jnp.dot(a_ref[:rows], w_bf[...], preferred_element_type=F32) + b_ref[...]

    _by_valid_rows(valid, y_ref, down_rows)


def _down(act, texp, nused, tvalid, w_down, b_down, layer):
    p_rows = act.shape[0]
    ntiles = p_rows // MOE_TM
    tn = DOWN_TN
    tile = lambda t, nu: jnp.minimum(t, nu[0] - 1)
    bd = b_down.reshape(N_LAYERS, N_EXP, 1, D)
    return pl.pallas_call(
        _down_body,
        grid_spec=pltpu.PrefetchScalarGridSpec(
            num_scalar_prefetch=3,
            grid=(D // tn, ntiles),
            in_specs=[pl.BlockSpec((MOE_TM, DFF), lambda n, t, te, nu, tv: (tile(t, nu), 0)),
                      pl.BlockSpec((None, None, DFF, tn),
                                   lambda n, t, te, nu, tv: (layer, te[tile(t, nu)], 0, n)),
                      pl.BlockSpec((None, None, 1, tn),
                                   lambda n, t, te, nu, tv: (layer, te[tile(t, nu)], 0, n))],
            out_specs=pl.BlockSpec((MOE_TM, tn), lambda n, t, te, nu, tv: (t, n)),
            scratch_shapes=[pltpu.VMEM((DFF, tn), BF16)]),
        out_shape=jax.ShapeDtypeStruct((p_rows, D), F32),
        compiler_params=_cp(2), name="moe_down",
    )(texp, nused, tvalid, act, w_down, bd)


def _combine_body(dest_hbm, ys_hbm, x_ref, wt_ref, gp_ref, gs_ref, *rest, ns, last):
    if last:
        g_ref, yp_ref, ysm_ref, dsm, buf, sem_s, sem_r = rest
    else:
        g_ref, shp_ref, scp_ref, shs_ref, scs_ref, o_ref, h_ref, dsm, buf, sem_s, sem_r = rest
    i = pl.program_id(0)
    slot = _route_chunk(dest_hbm, dsm, sem_s, i, pl.num_programs(0))

    def row_copy(r, k, d):
        return pltpu.make_async_copy(ys_hbm.at[pl.ds(d, 1), :], buf.at[k, pl.ds(r, 1), :], sem_r)

    def combined(rows, gate_ref):
        def issue(r, c):
            for k in range(TOPK):
                row_copy(r, k, dsm[slot + r * TOPK + k]).start()
            return c

        lax.fori_loop(0, rows, issue, 0)

        def drain(r, c):
            for k in range(TOPK):
                row_copy(0, 0, 0).wait()
            return c

        lax.fori_loop(0, rows, drain, 0)
        wt = wt_ref[:rows]
        moe = buf[0, :rows] * wt[:, 0:1]
        for k in range(1, TOPK):
            moe = moe + buf[k, :rows] * wt[:, k:k + 1]
        return x_ref[:rows] + gate_ref[...] * moe

    def prompt():
        x_new = combined(x_ref.shape[0], gp_ref)
        if last:
            yp_ref[...] = _normed(x_new, g_ref[...])
        else:
            o_ref[...] = x_new
            h_ref[...] = (_normed(x_new, g_ref[...]) * (1.0 + scp_ref[...]) + shp_ref[...]).astype(h_ref.dtype)

    def sample():
        x_new = combined(ns, gs_ref)
        if last:
            ysm_ref[...] = _normed(x_new, g_ref[...])
        else:
            o_ref[:ns] = x_new
            o_ref[ns:] = jnp.zeros_like(o_ref[ns:])
            h_ref[:ns] = (_normed(x_new, g_ref[...]) * (1.0 + scs_ref[...]) + shs_ref[...]).astype(h_ref.dtype)
            h_ref[ns:] = jnp.zeros_like(h_ref[ns:])

    _prompt_or_sample(i, pl.num_programs(0), prompt, sample)


def _combine(x_all, ys, dest_pad, wts, mods, layer, np_rows, ns, next_norm):
    nt = x_all.shape[0]
    assert np_rows % TOK_CHUNK == 0 and ns <= TOK_CHUNK
    npt = np_rows // TOK_CHUNK
    last = next_norm[1] is None
    any_spec = pl.BlockSpec(memory_space=pl.ANY)
    gp, gs = mods.specs(layer, 2, TOK_CHUNK)
    rows = lambda w: pl.BlockSpec((TOK_CHUNK, w), lambda i: (i, 0))
    in_specs = [any_spec, any_spec, rows(D), rows(TOPK), gp, gs]
    args = [dest_pad, ys, x_all, wts, *mods.args]
    if last:
        in_specs.append(pl.BlockSpec((1, D), lambda i: (0, 0)))
        args.append(next_norm[0].reshape(1, D))
        out_specs = [pl.BlockSpec((TOK_CHUNK, D), lambda i: (jnp.minimum(i, npt - 1), 0)),
                     pl.BlockSpec((ns, D), lambda i: (0, 0))]
        out_shape = [jax.ShapeDtypeStruct((np_rows, D), F32), jax.ShapeDtypeStruct((ns, D), F32)]
    else:
        g1, mods1 = next_norm
        shp, shs = mods1.specs(layer + 1, 0, TOK_CHUNK)
        scp, scs = mods1.specs(layer + 1, 1, TOK_CHUNK)
        in_specs += [pl.BlockSpec((None, 1, D), lambda i: (layer + 1, 0, 0)), shp, scp, shs, scs]
        args += [g1, mods1.m4, mods1.m4, mods1.m3, mods1.m3]
        out_specs = [rows(D), rows(D)]
        out_shape = [jax.ShapeDtypeStruct((nt, D), F32), jax.ShapeDtypeStruct((nt, D), BF16)]
    return pl.pallas_call(
        functools.partial(_combine_body, ns=ns, last=last),
        grid=(npt + 1,),
        in_specs=in_specs,
        out_specs=out_specs,
        out_shape=out_shape,
        scratch_shapes=[pltpu.SMEM((2 * ROUTE_CHUNK,), I32), pltpu.VMEM((TOPK, TOK_CHUNK, D), F32),
                        pltpu.SemaphoreType.DMA((2,)), pltpu.SemaphoreType.DMA(())],
        compiler_params=_cp(1), name="moe_combine",
    )(*args)


def _moe(x_all, g, mods, layer, router_w, router_b, w_gate_up, b_gate_up, w_down, b_down, np_rows, ns,
         next_norm):
    nt = x_all.shape[0]
    hp, idx, wts, rank, cnt = _route(x_all, g, mods, layer, router_w, router_b, np_rows, ns)
    counts = cnt[0, :N_EXP].astype(I32)
    padded = ((counts + MOE_TM - 1) // MOE_TM) * MOE_TM
    ends = jnp.cumsum(padded)
    gstart = ends - padded
    dest = gstart[idx] + rank
    n_entries = nt * TOPK
    n_chunks = -(-n_entries // ROUTE_CHUNK)
    dest_pad = jnp.zeros((n_chunks * ROUTE_CHUNK,), I32).at[:n_entries].set(dest.reshape(-1))
    ntiles = -(-n_entries // MOE_TM) + N_EXP
    tile_ends = ends // MOE_TM
    tile_ids = jnp.arange(ntiles, dtype=I32)
    texp = jnp.minimum(jnp.sum((tile_ids[:, None] >= tile_ends[None, :]).astype(I32), axis=1), N_EXP - 1)
    nused = tile_ends[-1:].astype(I32)
    last_tile = jnp.where(padded > 0, tile_ends - 1, -1).astype(I32)
    tvalid = jnp.clip((gstart + counts)[texp] - tile_ids * MOE_TM, 0, MOE_TM)
    tvalid = jnp.where(tile_ids < nused[0], tvalid, 0).astype(I32)
    xs = _dispatch(hp, dest_pad, last_tile, nused, ntiles)
    act = _gate_up(xs, texp, nused, tvalid, w_gate_up, b_gate_up, layer)
    ys = _down(act, texp, nused, tvalid, w_down, b_down, layer)
    return _combine(x_all, ys, dest_pad, wts, mods, layer, np_rows, ns, next_norm)


def kernel(x_prompt, x_sample, state_ret, cache_kv_w128_d1, cache_kv_w512_d4, cache_kv_w2048_d16, c_prompt, c_sample, norm1_g, ada1_w, ada1_b, w_in, w_out_ret, w_out_dil, w_out, norm2_g, ada2_w, ada2_b, router_w, router_b, w_gate_up, b_gate_up, w_down, b_down, final_norm_g):
    nb, seq, _ = x_prompt.shape
    ns, dec_seq, _ = x_sample.shape
    assert dec_seq == 1 and seq == 2048 and ns % 16 == 0
    past = cache_kv_w2048_d16.shape[2]
    np_rows = nb * seq
    assert np_rows % ns == 0
    caches = (cache_kv_w128_d1, cache_kv_w512_d4, cache_kv_w2048_d16)

    x_src = (x_prompt.reshape(np_rows, D), x_sample.reshape(ns, D), 0)
    nc = -(-(ns + nb) // 16) * 16
    c_all = jnp.zeros((nc, D), F32).at[:ns].set(c_sample).at[ns:ns + nb].set(c_prompt)
    mods1 = _Mods(_mods(c_all, ada1_w, ada1_b), ns, nb, seq)
    mods2 = _Mods(_mods(c_all, ada2_w, ada2_b), ns, nb, seq)
    g1 = norm1_g.reshape(N_LAYERS, 1, D)
    g2 = norm2_g.reshape(N_LAYERS, 1, D)

    ret_p, ret_s, projs, kv_s = [], [], [], [[] for _ in DIL]
    h = _norm_mod(x_src, g1, mods1, 0, np_rows, ns)
    for layer in range(N_LAYERS):
        proj = _proj(h, w_in, layer)
        projs.append(proj)
        ya_p, st_p = _ret_prompt(proj, nb, seq)
        ya_s, st_s = _ret_sample(proj, state_ret, layer, np_rows, ns, past)
        yb_p = _dil_prompt(proj, nb, seq)
        yb_s = _dil_sample(proj, caches, layer, np_rows, ns)
        x_all = _mix_out(x_src, ya_p, yb_p, ya_s, yb_s, proj, w_out_ret, w_out_dil, w_out, mods1, layer)
        last = layer == N_LAYERS - 1
        x_all, h = _moe(x_all, g2, mods2, layer, router_w, router_b, w_gate_up, b_gate_up, w_down, b_down,
                        np_rows, ns, (final_norm_g, None) if last else (g1, mods1))
        x_src = (x_all, x_all, np_rows // ns)
        ret_p.append(st_p)
        ret_s.append(st_s)
        for g in range(len(DIL)):
            c0 = OFF_DIL + g * 3 * DIL_W + DIL_W
            kv_s[g].append(proj[np_rows:, c0:c0 + 2 * DIL_W].reshape(ns, 1, 2, DIL_H, DIL_E))
    y_p, y_s = x_all, h
    kv_p = [_kv_out(projs, g, nb, seq) for g in range(len(DIL))]
    return (y_p.reshape(nb, seq, D), y_s.reshape(ns, 1, D),
            jnp.stack(ret_p), kv_p[0], kv_p[1], kv_p[2],
            jnp.stack(ret_s), jnp.stack(kv_s[0]), jnp.stack(kv_s[1]), jnp.stack(kv_s[2]))
```
